```python
import jax
import jax.numpy as jnp
from jax import lax
import numpy as np

D_MODEL = 1024
BATCH = 1
SEQ = 16384
DEPTH = 4

GRID_W = 64
CTX_LEN = 256
N_HEADS = 8
N_KV_HEADS = 2
HEAD_DIM = 64
Q_GROUP = N_HEADS // N_KV_HEADS
Q_W = N_HEADS * HEAD_DIM
KV_W = N_KV_HEADS * HEAD_DIM
ATTN_BLOCK = 128
ROPE_THETA = 10000.0
FNET_GROUPS = 4
FNET_GROUP_DIM = 64
FNET_WIDTH = FNET_GROUPS * FNET_GROUP_DIM
CONV_WIDTH = 256
CONV_KERNEL = 31
SC_WIDTH = 256
SC_KERNEL = 3
N_BRANCHES = 4
D_FF = 2816
N_MOD = 9
EPS = 1e-6
IN_SPLITS = (Q_W, KV_W, KV_W, FNET_WIDTH, 2 * CONV_WIDTH, 3 * SC_WIDTH, N_BRANCHES * D_MODEL)
IN_W = sum(IN_SPLITS)

kernel_name = "hybrid_parallel_branch_diffusion_block"


def rmsnorm(x, g):
    x32 = x.astype(jnp.float32)
    y = x32 * lax.rsqrt(jnp.mean(x32 * x32, axis=-1, keepdims=True) + EPS)
    return (y * g.astype(jnp.float32)).astype(x.dtype)


def layernorm(x, g, b):
    x32 = x.astype(jnp.float32)
    mu = jnp.mean(x32, axis=-1, keepdims=True)
    var = jnp.mean(jnp.square(x32 - mu), axis=-1, keepdims=True)
    y = (x32 - mu) * lax.rsqrt(var + EPS)
    return (y * g.astype(jnp.float32) + b.astype(jnp.float32)).astype(x.dtype)


def split_in(p):
    idx = [int(i) for i in np.cumsum(IN_SPLITS)[:-1]]
    return jnp.split(p, idx, axis=-1)


def axial_rope_tables(n_tokens):
    n_rows = n_tokens // GRID_W
    row = jnp.broadcast_to(jnp.arange(n_rows)[:, None], (n_rows, GRID_W)).reshape(-1)
    col = jnp.broadcast_to(jnp.arange(GRID_W)[None, :], (n_rows, GRID_W)).reshape(-1)
    axis_dim = HEAD_DIM // 2
    inv_freq = ROPE_THETA ** (-jnp.arange(0, axis_dim, 2, dtype=jnp.float32) / axis_dim)
    pos = jnp.stack([row, col], axis=-1).astype(jnp.float32)
    ang = pos[:, :, None] * inv_freq
    return jnp.cos(ang), jnp.sin(ang)


def apply_axial_rope(x, cos, sin):
    B, L, H, _ = x.shape
    xr = x.astype(jnp.float32).reshape(B, L, H, 2, 2, HEAD_DIM // 4)
    x1, x2 = xr[..., 0, :], xr[..., 1, :]
    cs = cos[None, :, None]
    sn = sin[None, :, None]
    out = jnp.stack([x1 * cs - x2 * sn, x1 * sn + x2 * cs], axis=-2)
    return out.reshape(x.shape).astype(x.dtype)


def to_heads(t, n_heads):
    B, L, _ = t.shape
    return t.reshape(B, L, n_heads, HEAD_DIM)


def attend(q, k, v):
    s = jnp.einsum('bqkgd,bskd->bkgqs', q, k).astype(jnp.float32) * (HEAD_DIM ** -0.5)
    p = jax.nn.softmax(s, axis=-1).astype(v.dtype)
    return jnp.einsum('bkgqs,bskd->bqkgd', p, v)


def latent_attention(q, k, v, k_ctx, v_ctx):
    B, L = q.shape[:2]
    keys = jnp.concatenate([k_ctx, k], axis=1)
    vals = jnp.concatenate([v_ctx, v], axis=1)
    n_blk = L // ATTN_BLOCK
    qb = q.reshape(B, n_blk, ATTN_BLOCK, N_KV_HEADS, Q_GROUP, HEAD_DIM).transpose(1, 0, 2, 3, 4, 5)
    ob = lax.map(lambda qq: attend(qq, keys, vals), qb)
    return ob.transpose(1, 0, 2, 3, 4, 5).reshape(B, L, Q_W)


def context_attention(q, k, v):
    B, L = q.shape[:2]
    o = attend(q.reshape(B, L, N_KV_HEADS, Q_GROUP, HEAD_DIM), k, v)
    return o.reshape(B, L, Q_W)


def depthwise_conv(x, w):
    k_w = w.shape[0]
    pad = (k_w - 1) // 2
    return lax.conv_general_dilated(
        x, w[:, None, :].astype(x.dtype), window_strides=(1,), padding=[(pad, pad)],
        dimension_numbers=('NWC', 'WIO', 'NWC'), feature_group_count=x.shape[-1])


def fourier_mix(f):
    B, L, _ = f.shape
    fg = f.astype(jnp.float32).reshape(B, L, FNET_GROUPS, FNET_GROUP_DIM)
    y = jnp.fft.fft2(fg, axes=(1, 3), norm='ortho').real
    return y.reshape(B, L, FNET_WIDTH).astype(f.dtype)


def conformer_conv(a, dw_w, dw_b, ln_g, ln_b):
    a1, a2 = jnp.split(a, 2, axis=-1)
    h = a1 * jax.nn.sigmoid(a2)
    h = depthwise_conv(h, dw_w) + dw_b
    h = layernorm(h, ln_g, ln_b)
    return jax.nn.silu(h)


def short_gated_conv(s, w):
    gb, gc, h = jnp.split(s, 3, axis=-1)
    return gb * depthwise_conv(gc * h, w)


def merge_branches(attn, f, cg, sc, g, lw):
    B, L, _ = attn.shape
    y_a = attn @ lw['w_attn_out']
    y_b = fourier_mix(f) @ lw['w_fnet']
    y_c = conformer_conv(cg, lw['conv_dw_w'], lw['conv_dw_b'], lw['conv_ln_g'], lw['conv_ln_b']) @ lw['w_conv_out']
    y_d = short_gated_conv(sc, lw['sc_conv_w']) @ lw['w_sc_out']
    gates = jax.nn.sigmoid(g + lw['b_gate']).reshape(B, L, N_BRANCHES, D_MODEL)
    merged = gates[:, :, 0] * y_a + gates[:, :, 1] * y_b + gates[:, :, 2] * y_c + gates[:, :, 3] * y_d
    return merged @ lw['w_o']


def modulated_norm(x, mod, slot, g):
    return rmsnorm(x, g) * (1 + mod[:, :, 3 * slot + 1]) + mod[:, :, 3 * slot]


def ffn_half(x, mod, slot, g, w13, w2):
    h = modulated_norm(x, mod, slot, g)
    a, b = jnp.split(h @ w13, 2, axis=-1)
    return x + 0.5 * mod[:, :, 3 * slot + 2] * ((jax.nn.silu(a) * b) @ w2)


def setup_inputs(seed: int = 0) -> dict:
    key = jax.random.key(seed)
    ks = jax.random.split(key, 24)
    nrm = jax.random.normal
    f32 = jnp.float32
    D = D_MODEL
    return {
        'x': nrm(ks[0], (BATCH, SEQ, D), f32),
        'c': nrm(ks[1], (BATCH, D), f32),
        'ctx': nrm(ks[2], (BATCH, CTX_LEN, D), f32),
        'c_ctx': nrm(ks[3], (D,), f32),
        'w_ada': nrm(ks[4], (DEPTH, D, N_MOD * D), f32) * (0.5 * D ** -0.5),
        'b_ada': nrm(ks[5], (DEPTH, N_MOD * D), f32) * 0.01,
        'norm_g': 1.0 + 0.01 * nrm(ks[6], (DEPTH, 3, D), f32),
        'ffn_w13': nrm(ks[7], (DEPTH, 2, D, 2 * D_FF), f32) * D ** -0.5,
        'ffn_w2': nrm(ks[8], (DEPTH, 2, D_FF, D), f32) * D_FF ** -0.5,
        'w_in': nrm(ks[9], (DEPTH, D, IN_W), f32) * D ** -0.5,
        'b_gate': nrm(ks[10], (DEPTH, N_BRANCHES * D), f32) * 0.01,
        'q_norm_g': 1.0 + 0.01 * nrm(ks[11], (DEPTH, HEAD_DIM), f32),
        'k_norm_g': 1.0 + 0.01 * nrm(ks[12], (DEPTH, HEAD_DIM), f32),
        'w_attn_out': nrm(ks[13], (DEPTH, Q_W, D), f32) * Q_W ** -0.5,
        'w_fnet': nrm(ks[14], (DEPTH, FNET_WIDTH, D), f32) * FNET_WIDTH ** -0.5,
        'conv_dw_w': nrm(ks[15], (DEPTH, CONV_KERNEL, CONV_WIDTH), f32) * CONV_KERNEL ** -0.5,
        'conv_dw_b': nrm(ks[16], (DEPTH, CONV_WIDTH), f32) * 0.01,
        'conv_ln_g': 1.0 + 0.01 * nrm(ks[17], (DEPTH, CONV_WIDTH), f32),
        'conv_ln_b': nrm(ks[18], (DEPTH, CONV_WIDTH), f32) * 0.01,
        'w_conv_out': nrm(ks[19], (DEPTH, CONV_WIDTH, D), f32) * CONV_WIDTH ** -0.5,
        'sc_conv_w': nrm(ks[20], (DEPTH, SC_KERNEL, SC_WIDTH), f32) * SC_KERNEL ** -0.5,
        'w_sc_out': nrm(ks[21], (DEPTH, SC_WIDTH, D), f32) * SC_WIDTH ** -0.5,
        'w_o': nrm(ks[22], (DEPTH, D, D), f32) * D ** -0.5,
        'final_norm_g': 1.0 + 0.01 * nrm(ks[23], (D,), f32),
    }


def reference(x, c, ctx, c_ctx, w_ada, b_ada, norm_g, ffn_w13, ffn_w2, w_in, b_gate,
              q_norm_g, k_norm_g, w_attn_out, w_fnet, conv_dw_w, conv_dw_b, conv_ln_g,
              conv_ln_b, w_conv_out, sc_conv_w, w_sc_out, w_o, final_norm_g):
    B, L, D = x.shape
    cos, sin = axial_rope_tables(L)
    cx = ctx
    for l in range(DEPTH):
        last = l == DEPTH - 1
        lw = {
            'w_attn_out': w_attn_out[l], 'w_fnet': w_fnet[l], 'conv_dw_w': conv_dw_w[l],
            'conv_dw_b': conv_dw_b[l], 'conv_ln_g': conv_ln_g[l], 'conv_ln_b': conv_ln_b[l],
            'w_conv_out': w_conv_out[l], 'sc_conv_w': sc_conv_w[l], 'w_sc_out': w_sc_out[l],
            'b_gate': b_gate[l], 'w_o': w_o[l],
        }
        mod_x = (jax.nn.silu(c) @ w_ada[l] + b_ada[l]).reshape(B, 1, N_MOD, D)
        mod_c = (jax.nn.silu(c_ctx) @ w_ada[l] + b_ada[l]).reshape(1, 1, N_MOD, D)

        x = ffn_half(x, mod_x, 0, norm_g[l, 0], ffn_w13[l, 0], ffn_w2[l, 0])
        cx = ffn_half(cx, mod_c, 0, norm_g[l, 0], ffn_w13[l, 0], ffn_w2[l, 0])

        u_x = modulated_norm(x, mod_x, 1, norm_g[l, 1])
        u_c = modulated_norm(cx, mod_c, 1, norm_g[l, 1])
        q_x, k_x, v_x, f_x, cg_x, sc_x, g_x = split_in(u_x @ w_in[l])
        if last:
            k_c, v_c = jnp.split(u_c @ w_in[l, :, Q_W:Q_W + 2 * KV_W], 2, axis=-1)
        else:
            q_c, k_c, v_c, f_c, cg_c, sc_c, g_c = split_in(u_c @ w_in[l])
        k_c = rmsnorm(to_heads(k_c, N_KV_HEADS), k_norm_g[l])
        v_c = to_heads(v_c, N_KV_HEADS)

        qh = apply_axial_rope(rmsnorm(to_heads(q_x, N_HEADS), q_norm_g[l]), cos, sin)
        kh = apply_axial_rope(rmsnorm(to_heads(k_x, N_KV_HEADS), k_norm_g[l]), cos, sin)
        attn_x = latent_attention(qh, kh, to_heads(v_x, N_KV_HEADS), k_c, v_c)
        x = x + mod_x[:, :, 5] * merge_branches(attn_x, f_x, cg_x, sc_x, g_x, lw)

        if not last:
            qc = rmsnorm(to_heads(q_c, N_HEADS), q_norm_g[l])
            attn_c = context_attention(qc, k_c, v_c)
            cx = cx + mod_c[:, :, 5] * merge_branches(attn_c, f_c, cg_c, sc_c, g_c, lw)

        x = ffn_half(x, mod_x, 2, norm_g[l, 2], ffn_w13[l, 1], ffn_w2[l, 1])
        if not last:
            cx = ffn_half(cx, mod_c, 2, norm_g[l, 2], ffn_w13[l, 1], ffn_w2[l, 1])

    return rmsnorm(x, final_norm_g)
```

```python
import functools
import math

import numpy as np
import jax
import jax.numpy as jnp
from jax import lax
from jax.experimental import pallas as pl
from jax.experimental.pallas import tpu as pltpu

F32 = jnp.float32
BF16 = jnp.bfloat16

D_MODEL = 1024
GRID_W = 64
N_HEADS = 8
N_KV_HEADS = 2
HEAD_DIM = 64
Q_GROUP = N_HEADS // N_KV_HEADS
Q_W = N_HEADS * HEAD_DIM
KV_W = N_KV_HEADS * HEAD_DIM
ROPE_THETA = 10000.0
FNET_GROUPS = 4
FNET_GROUP_DIM = 64
FNET_WIDTH = FNET_GROUPS * FNET_GROUP_DIM
CONV_WIDTH = 256
CONV_KERNEL = 31
SC_WIDTH = 256
SC_KERNEL = 3
N_BRANCHES = 4
D_FF = 2816
N_MOD = 9
EPS = 1e-6
QKV_W = Q_W + 2 * KV_W
REST_W = FNET_WIDTH + 2 * CONV_WIDTH + 3 * SC_WIDTH
HALO = 16
FF_CHUNK = 256
V7X_VMEM_LIMIT_BYTES = 56 * 1024 * 1024


def _tile(n, target):
    if n <= target:
        return n
    t = (target // 128) * 128
    while t >= 128:
        if n % t == 0:
            return t
        t -= 128
    return n


def _params(sem):
    return pltpu.CompilerParams(dimension_semantics=sem, vmem_limit_bytes=V7X_VMEM_LIMIT_BYTES)


def _dot(a, b):
    return jnp.dot(a, b, preferred_element_type=F32)


def _dot_nt(a, b):
    return lax.dot_general(a, b, (((1,), (1,)), ((), ())), preferred_element_type=F32)


def _modnorm(x, g, shift, scale):
    y = x * lax.rsqrt(jnp.mean(x * x, axis=-1, keepdims=True) + EPS)
    return (y * g) * (1.0 + scale) + shift


def _silu(x):
    return x * jax.nn.sigmoid(x)


def _mod_kernel(c_ref, w_ref, b_ref, o_ref):
    a = _silu(c_ref[...])
    o_ref[0] = _dot(a.astype(BF16), w_ref[0].astype(BF16)) + b_ref[0]


def _mod_call(cc, w_ada, b_ada):
    depth, d, n = w_ada.shape
    tn = 1152
    return pl.pallas_call(
        _mod_kernel,
        grid=(depth, n // tn),
        in_specs=[
            pl.BlockSpec((8, d), lambda l, j: (0, 0)),
            pl.BlockSpec((1, d, tn), lambda l, j: (l, 0, j)),
            pl.BlockSpec((1, 1, tn), lambda l, j: (l, 0, j)),
        ],
        out_specs=pl.BlockSpec((1, 8, tn), lambda l, j: (l, 0, j)),
        out_shape=jax.ShapeDtypeStruct((depth, 8, n), F32),
        compiler_params=_params(("parallel", "parallel")),
        name="adaln_mod",
    )(cc, w_ada, b_ada.reshape(depth, 1, n))


def _ffn_kernel(x_ref, mod_ref, g_ref, w1_ref, w3_ref, w2_ref, fg_ref, o_ref, h_ref, acc_ref, *, final):
    j = pl.program_id(1)

    @pl.when(j == 0)
    def _():
        h = _modnorm(x_ref[...], g_ref[...], mod_ref[0:1, :], mod_ref[1:2, :])
        h_ref[...] = h.astype(BF16)
        acc_ref[...] = jnp.zeros_like(acc_ref)

    h = h_ref[...]
    a = _dot(h, w1_ref[...])
    b = _dot(h, w3_ref[...])
    acc_ref[...] += _dot((_silu(a) * b).astype(BF16), w2_ref[...])

    @pl.when(j == pl.num_programs(1) - 1)
    def _():
        y = x_ref[...] + (0.5 * mod_ref[2:3, :]) * acc_ref[...]
        if final:
            y = (y * lax.rsqrt(jnp.mean(y * y, axis=-1, keepdims=True) + EPS)) * fg_ref[...]
        o_ref[...] = y


def _ffn_call(x, mod3, g, w13, w2, fg, *, final, tm_target):
    n, d = x.shape
    tm = _tile(n, tm_target)
    nf = D_FF // FF_CHUNK
    return pl.pallas_call(
        functools.partial(_ffn_kernel, final=final),
        grid=(n // tm, nf),
        in_specs=[
            pl.BlockSpec((tm, d), lambda i, j: (i, 0)),
            pl.BlockSpec((3, d), lambda i, j: (0, 0)),
            pl.BlockSpec((1, d), lambda i, j: (0, 0)),
            pl.BlockSpec((d, FF_CHUNK), lambda i, j: (0, j)),
            pl.BlockSpec((d, FF_CHUNK), lambda i, j: (0, j + nf)),
            pl.BlockSpec((FF_CHUNK, d), lambda i, j: (j, 0)),
            pl.BlockSpec((1, d), lambda i, j: (0, 0)),
        ],
        out_specs=pl.BlockSpec((tm, d), lambda i, j: (i, 0)),
        out_shape=jax.ShapeDtypeStruct((n, d), F32),
        scratch_shapes=[pltpu.VMEM((tm, d), BF16), pltpu.VMEM((tm, d), F32)],
        compiler_params=_params(("parallel", "arbitrary")),
        name="ffn_half",
    )(x, mod3, g, w13, w13, w2, fg)


def _head_norm_rope(t, gain, cos, sin, scale):
    ms = jnp.sum(t * t, axis=0, keepdims=True) * (1.0 / HEAD_DIM)
    t = (t * lax.rsqrt(ms + EPS)) * gain
    quarter = HEAD_DIM // 4
    parts = []
    for axis in range(2):
        x1 = t[2 * axis * quarter:(2 * axis + 1) * quarter]
        x2 = t[(2 * axis + 1) * quarter:(2 * axis + 2) * quarter]
        c = cos[axis * quarter:(axis + 1) * quarter]
        s = sin[axis * quarter:(axis + 1) * quarter]
        parts += [x1 * c - x2 * s, x1 * s + x2 * c]
    out = jnp.concatenate(parts, axis=0)
    return out * scale if scale != 1.0 else out


def _inproj_kernel(x_ref, mod_ref, g_ref, wqkv_ref, wrest_ref, qg_ref, kg_ref, cos_ref, sin_ref,
                   q_ref, k_ref, v_ref, f_ref, h_ref, s_ref, gb_ref):
    u = _modnorm(x_ref[...], g_ref[...], mod_ref[0:1, :], mod_ref[1:2, :]).astype(BF16)
    qkv = _dot_nt(wqkv_ref[...], u)
    cos = cos_ref[...]
    sin = sin_ref[...]
    zeros = jnp.zeros((HEAD_DIM, qkv.shape[1]), F32)
    for h in range(N_HEADS):
        t = _head_norm_rope(qkv[h * HEAD_DIM:(h + 1) * HEAD_DIM], qg_ref[...], cos, sin, HEAD_DIM ** -0.5)
        blk = [t, zeros] if h // Q_GROUP == 0 else [zeros, t]
        q_ref[h * KV_W:(h + 1) * KV_W, :] = jnp.concatenate(blk, axis=0).astype(BF16)
    kt = [_head_norm_rope(qkv[Q_W + h * HEAD_DIM:Q_W + (h + 1) * HEAD_DIM], kg_ref[...], cos, sin, 1.0)
          for h in range(N_KV_HEADS)]
    k_ref[...] = jnp.concatenate(kt, axis=0).T.astype(BF16)
    v_ref[...] = qkv[Q_W + KV_W:, :].astype(BF16)

    r = _dot(u, wrest_ref[...])
    w = FNET_WIDTH
    f_ref[...] = r[:, 0:w]
    h_ref[...] = r[:, w:2 * w] * jax.nn.sigmoid(r[:, 2 * w:3 * w])
    gb_ref[...] = r[:, 3 * w:4 * w]
    s_ref[...] = r[:, 4 * w:5 * w] * r[:, 5 * w:6 * w]


def _inproj_call(x, mod2, g, wqkv_t, wrest, qg, kg, cos_t, sin_t, *, tm_target):
    n, d = x.shape
    tm = _tile(n, tm_target)
    row = lambda i: (i, 0)
    col = lambda i: (0, i)
    fix = lambda i: (0, 0)
    w = FNET_WIDTH
    return pl.pallas_call(
        _inproj_kernel,
        grid=(n // tm,),
        in_specs=[
            pl.BlockSpec((tm, d), row),
            pl.BlockSpec((2, d), fix),
            pl.BlockSpec((1, d), fix),
            pl.BlockSpec((QKV_W, d), fix),
            pl.BlockSpec((d, REST_W), fix),
            pl.BlockSpec((HEAD_DIM, 1), fix),
            pl.BlockSpec((HEAD_DIM, 1), fix),
            pl.BlockSpec((HEAD_DIM // 2, tm), col),
            pl.BlockSpec((HEAD_DIM // 2, tm), col),
        ],
        out_specs=[
            pl.BlockSpec((N_HEADS * KV_W, tm), col),
            pl.BlockSpec((tm, KV_W), row),
            pl.BlockSpec((KV_W, tm), col),
            pl.BlockSpec((tm, w), row),
            pl.BlockSpec((tm, w), row),
            pl.BlockSpec((tm, w), row),
            pl.BlockSpec((tm, w), row),
        ],
        out_shape=[
            jax.ShapeDtypeStruct((N_HEADS * KV_W, n), BF16),
            jax.ShapeDtypeStruct((n, KV_W), BF16),
            jax.ShapeDtypeStruct((KV_W, n), BF16),
            jax.ShapeDtypeStruct((n, w), F32),
            jax.ShapeDtypeStruct((n, w), F32),
            jax.ShapeDtypeStruct((n, w), F32),
            jax.ShapeDtypeStruct((n, w), F32),
        ],
        compiler_params=_params(("parallel",)),
        name="in_proj",
    )(x, mod2, g, wqkv_t, wrest, qg, kg, cos_t, sin_t)


def _attn_kernel(q_ref, k_ref, v_ref, o_ref, m_ref, l_ref, acc_ref):
    j = pl.program_id(2)

    @pl.when(j == 0)
    def _():
        m_ref[...] = jnp.full_like(m_ref, -jnp.inf)
        l_ref[...] = jnp.zeros_like(l_ref)
        acc_ref[...] = jnp.zeros_like(acc_ref)

    k = k_ref[...]
    v = v_ref[...]
    for h in range(Q_GROUP):
        s = _dot(k, q_ref[h * KV_W:(h + 1) * KV_W, :])
        m_old = m_ref[h:h + 1, :]
        m_new = jnp.maximum(m_old, jnp.max(s, axis=0, keepdims=True))
        alpha = jnp.exp(m_old - m_new)
        p = jnp.exp(s - m_new)
        l_ref[h:h + 1, :] = alpha * l_ref[h:h + 1, :] + jnp.sum(p, axis=0, keepdims=True)
        rows = slice(h * HEAD_DIM, (h + 1) * HEAD_DIM)
        acc_ref[rows, :] = alpha * acc_ref[rows, :] + _dot(v, p.astype(BF16))
        m_ref[h:h + 1, :] = m_new

    @pl.when(j == pl.num_programs(2) - 1)
    def _():
        outs = [acc_ref[h * HEAD_DIM:(h + 1) * HEAD_DIM, :] / l_ref[h:h + 1, :] for h in range(Q_GROUP)]
        o_ref[...] = jnp.concatenate(outs, axis=0).T.astype(BF16)


def _attn_call(q_t, k, v_t, *, tq_target, tk_target):
    nq = q_t.shape[1]
    nk = k.shape[0]
    tq = _tile(nq, tq_target)
    tk = _tile(nk, tk_target)
    gw = Q_GROUP * HEAD_DIM
    return pl.pallas_call(
        _attn_kernel,
        grid=(N_KV_HEADS, nq // tq, nk // tk),
        in_specs=[
            pl.BlockSpec((Q_GROUP * KV_W, tq), lambda g, i, j: (g, i)),
            pl.BlockSpec((tk, KV_W), lambda g, i, j: (j, 0)),
            pl.BlockSpec((HEAD_DIM, tk), lambda g, i, j: (g, j)),
        ],
        out_specs=pl.BlockSpec((tq, gw), lambda g, i, j: (i, g)),
        out_shape=jax.ShapeDtypeStruct((nq, Q_W), BF16),
        scratch_shapes=[pltpu.VMEM((8, tq), F32), pltpu.VMEM((8, tq), F32), pltpu.VMEM((gw, tq), F32)],
        compiler_params=_params(("parallel", "parallel", "arbitrary")),
        name="attention",
    )(q_t, k, v_t)


def _split_len(n):
    b = 1 << (int(math.log2(n)) // 2)
    return n // b, b


def _channel_dft(scale):
    c = np.arange(FNET_GROUP_DIM)
    ang = 2.0 * np.pi * np.outer(c, c) / FNET_GROUP_DIM
    eye = np.eye(FNET_GROUPS)
    return np.concatenate([np.kron(eye, np.cos(ang)), -np.kron(eye, np.sin(ang))], axis=1) * scale


def _stage1_tables(n):
    a, b = _split_len(n)
    c = np.arange(a)[None, :, None]
    pos = b * np.arange(a)[None, None, :] + np.arange(b)[:, None, None]
    ang = 2.0 * np.pi * ((c * pos) % n) / n
    co, si = np.cos(ang), np.sin(ang)
    return np.concatenate([np.concatenate([co, si], axis=2), np.concatenate([-si, co], axis=2)], axis=1)


def _stage2_table(n):
    a, b = _split_len(n)
    ang = 2.0 * np.pi * np.outer(np.arange(b), np.arange(b)) / b
    eye = np.eye(8)
    return np.concatenate([np.kron(np.cos(ang), eye), np.kron(np.sin(ang), eye)], axis=1)


def _dft_tables(n):
    ang = 2.0 * np.pi * np.outer(np.arange(n), np.arange(n)) / n
    return np.concatenate([np.cos(ang), np.sin(ang)], axis=1)


def _fft1_kernel(f_ref, t_ref, d_ref, o_ref, *, nb):
    w = FNET_WIDTH
    for b in range(nb):
        z = _dot(f_ref[:, b * w:(b + 1) * w].astype(BF16), d_ref[...])
        zz = jnp.concatenate([z[:, :w], z[:, w:]], axis=0).astype(BF16)
        o_ref[b] = _dot(t_ref[b], zz)


def _fft2_kernel(re_ref, im_ref, t_ref, o_ref):
    nb, _, w = re_ref.shape
    st = jnp.concatenate([re_ref[...].reshape(nb * 8, w), im_ref[...].reshape(nb * 8, w)], axis=0)
    o_ref[...] = _dot(t_ref[...], st.astype(BF16)).reshape(nb, 8, w)


def _fft_small_kernel(f_ref, t_ref, d_ref, o_ref):
    w = FNET_WIDTH
    z = _dot(f_ref[...].astype(BF16), d_ref[...])
    zz = jnp.concatenate([z[:, :w], z[:, w:]], axis=0).astype(BF16)
    o_ref[...] = _dot(t_ref[...], zz)


def _fourier_call(f, consts):
    n, w = f.shape
    if "t1" not in consts:
        return pl.pallas_call(
            _fft_small_kernel,
            out_shape=jax.ShapeDtypeStruct((n, w), F32),
            compiler_params=pltpu.CompilerParams(vmem_limit_bytes=V7X_VMEM_LIMIT_BYTES),
            name="fourier_small",
        )(f, consts["t"], consts["d"])
    a, b = _split_len(n)
    nb = 8
    mid = pl.pallas_call(
        functools.partial(_fft1_kernel, nb=nb),
        grid=(b // nb,),
        in_specs=[
            pl.BlockSpec((a, nb * w), lambda i: (0, i)),
            pl.BlockSpec((nb, 2 * a, 2 * a), lambda i: (i, 0, 0)),
            pl.BlockSpec((w, 2 * w), lambda i: (0, 0)),
        ],
        out_specs=pl.BlockSpec((nb, 2 * a, w), lambda i: (i, 0, 0)),
        out_shape=jax.ShapeDtypeStruct((b, 2 * a, w), F32),
        compiler_params=_params(("parallel",)),
        name="fourier_stage1",
    )(f.reshape(a, b * w), consts["t1"], consts["d"])
    mid = mid.reshape(b, 2, a, w)
    out = pl.pallas_call(
        _fft2_kernel,
        grid=(a // 8,),
        in_specs=[
            pl.BlockSpec((b, None, 8, w), lambda i: (0, 0, i, 0)),
            pl.BlockSpec((b, None, 8, w), lambda i: (0, 1, i, 0)),
            pl.BlockSpec((8 * b, 16 * b), lambda i: (0, 0)),
        ],
        out_specs=pl.BlockSpec((b, 8, w), lambda i: (0, i, 0)),
        out_shape=jax.ShapeDtypeStruct((b, a, w), F32),
        compiler_params=_params(("parallel",)),
        name="fourier_stage2",
    )(mid, mid, consts["t2"])
    return out.reshape(n, w)


def _fourier_consts(n):
    scale = 1.0 / math.sqrt(n * FNET_GROUP_DIM)
    d = jnp.asarray(_channel_dft(scale), BF16)
    if n <= 512:
        return {"d": d, "t": jnp.asarray(_dft_tables(n), BF16)}
    return {"d": d, "t1": jnp.asarray(_stage1_tables(n), BF16), "t2": jnp.asarray(_stage2_table(n), BF16)}


def _fill_window(win_ref, prev_ref, cur_ref, next_ref, tm):
    i = pl.program_id(0)
    last = pl.num_programs(0) - 1
    win_ref[0:HALO, :] = jnp.where(i > 0, prev_ref[tm - HALO:tm, :], 0.0)
    win_ref[HALO:HALO + tm, :] = cur_ref[...]
    win_ref[HALO + tm:HALO + tm + HALO, :] = jnp.where(i < last, next_ref[0:HALO, :], 0.0)


def _depthwise(win_ref, w_ref, taps, tm):
    pad = (taps - 1) // 2
    acc = None
    for t in range(taps):
        term = win_ref[pl.ds(HALO - pad + t, tm), :] * w_ref[t:t + 1, :]
        acc = term if acc is None else acc + term
    return acc


def _merge_kernel(x_ref, mod_ref, g_ref, attn_ref, fn_ref, hp_ref, hc_ref, hn_ref, sp_ref, sc_ref, sn_ref,
                  gb_ref, wg_ref, bg_ref, wa_ref, wf_ref, wc_ref, ws_ref, wo_ref,
                  dww_ref, dwb_ref, lng_ref, lnb_ref, scw_ref, o_ref, hwin_ref, swin_ref):
    tm, d = x_ref.shape
    x = x_ref[...]
    u = _modnorm(x, g_ref[...], mod_ref[0:1, :], mod_ref[1:2, :]).astype(BF16)

    _fill_window(hwin_ref, hp_ref, hc_ref, hn_ref, tm)
    _fill_window(swin_ref, sp_ref, sc_ref, sn_ref, tm)
    c = _depthwise(hwin_ref, dww_ref, CONV_KERNEL, tm) + dwb_ref[...]
    mu = jnp.mean(c, axis=-1, keepdims=True)
    cc = c - mu
    var = jnp.mean(cc * cc, axis=-1, keepdims=True)
    conf = _silu((cc * lax.rsqrt(var + EPS)) * lng_ref[...] + lnb_ref[...])
    short = gb_ref[...] * _depthwise(swin_ref, scw_ref, SC_KERNEL, tm)

    branches = (
        (attn_ref[...], wa_ref),
        (fn_ref[...].astype(BF16), wf_ref),
        (conf.astype(BF16), wc_ref),
        (short.astype(BF16), ws_ref),
    )
    merged = None
    for i, (val, w_ref) in enumerate(branches):
        cols = slice(i * d, (i + 1) * d)
        gate = jax.nn.sigmoid(_dot(u, wg_ref[:, cols]) + bg_ref[:, cols])
        term = gate * _dot(val, w_ref[...])
        merged = term if merged is None else merged + term
    o_ref[...] = x + mod_ref[2:3, :] * _dot(merged.astype(BF16), wo_ref[...])


def _merge_call(x, mod3, g, attn, fn, hglu, sprod, gb, lw, *, tm_target):
    n, d = x.shape
    tm = _tile(n, tm_target)
    nt = n // tm
    row = lambda i: (i, 0)
    prev = lambda i: (jnp.maximum(i - 1, 0), 0)
    nxt = lambda i: (jnp.minimum(i + 1, nt - 1), 0)
    fix = lambda i: (0, 0)
    w = CONV_WIDTH

    def whole(arr):
        return pl.BlockSpec(arr.shape, fix, pipeline_mode=pl.Buffered(1))

    weights = (lw["wg"], lw["bg"], lw["wa"], lw["wf"], lw["wc"], lw["ws"], lw["wo"],
               lw["dww"], lw["dwb"], lw["lng"], lw["lnb"], lw["scw"])
    return pl.pallas_call(
        _merge_kernel,
        grid=(nt,),
        in_specs=[
            pl.BlockSpec((tm, d), row),
            pl.BlockSpec((3, d), fix),
            pl.BlockSpec((1, d), fix),
            pl.BlockSpec((tm, Q_W), row),
            pl.BlockSpec((tm, FNET_WIDTH), row),
            pl.BlockSpec((tm, w), prev), pl.BlockSpec((tm, w), row), pl.BlockSpec((tm, w), nxt),
            pl.BlockSpec((tm, w), prev), pl.BlockSpec((tm, w), row), pl.BlockSpec((tm, w), nxt),
            pl.BlockSpec((tm, w), row),
        ] + [whole(a) for a in weights],
        out_specs=pl.BlockSpec((tm, d), row),
        out_shape=jax.ShapeDtypeStruct((n, d), F32),
        scratch_shapes=[pltpu.VMEM((tm + 2 * HALO, w), F32), pltpu.VMEM((tm + 2 * HALO, w), F32)],
        compiler_params=_params(("arbitrary",)),
        name="mix_merge",
    )(x, mod3, g, attn, fn, hglu, hglu, hglu, sprod, sprod, sprod, gb, *weights)


def _rope_tables_t(n_tokens):
    n_rows = n_tokens // GRID_W
    row = jnp.broadcast_to(jnp.arange(n_rows)[:, None], (n_rows, GRID_W)).reshape(-1)
    col = jnp.broadcast_to(jnp.arange(GRID_W)[None, :], (n_rows, GRID_W)).reshape(-1)
    axis_dim = HEAD_DIM // 2
    inv_freq = ROPE_THETA ** (-jnp.arange(0, axis_dim, 2, dtype=F32) / axis_dim)
    pos = jnp.stack([row, col], axis=-1).astype(F32)
    ang = pos[:, :, None] * inv_freq
    ang_t = ang.reshape(n_tokens, axis_dim).T
    return jnp.cos(ang_t), jnp.sin(ang_t)


def kernel(x, c, ctx, c_ctx, w_ada, b_ada, norm_g, ffn_w13, ffn_w2, w_in, b_gate, q_norm_g, k_norm_g,
           w_attn_out, w_fnet, conv_dw_w, conv_dw_b, conv_ln_g, conv_ln_b, w_conv_out, sc_conv_w,
           w_sc_out, w_o, final_norm_g):
    batch, seq, d = x.shape
    assert batch == 1 and d == D_MODEL
    n_ctx = ctx.shape[1]
    depth = w_ada.shape[0]

    xs = x[0]
    cs = ctx[0]
    cos_x, sin_x = _rope_tables_t(seq)
    cos_c = jnp.ones((HEAD_DIM // 2, n_ctx), F32)
    sin_c = jnp.zeros((HEAD_DIM // 2, n_ctx), F32)
    fc_x = _fourier_consts(seq)
    fc_c = _fourier_consts(n_ctx)

    cc = jnp.zeros((8, d), F32).at[0].set(c[0]).at[1].set(c_ctx)
    mod = _mod_call(cc, w_ada, b_ada)
    mod = mod.reshape(depth, 8, N_MOD, d)

    w13 = ffn_w13.astype(BF16)
    w2 = ffn_w2.astype(BF16)
    wqkv_t = jnp.swapaxes(w_in[:, :, :QKV_W], 1, 2).astype(BF16)
    wrest = w_in[:, :, QKV_W:QKV_W + REST_W].astype(BF16)
    wg = w_in[:, :, QKV_W + REST_W:].astype(BF16)
    fg = final_norm_g.reshape(1, d)

    for l in range(depth):
        last = l == depth - 1
        mod_x, mod_c = mod[l, 0], mod[l, 1]
        lw = {
            "wg": wg[l], "bg": b_gate[l].reshape(1, -1),
            "wa": w_attn_out[l].astype(BF16), "wf": w_fnet[l].astype(BF16),
            "wc": w_conv_out[l].astype(BF16), "ws": w_sc_out[l].astype(BF16), "wo": w_o[l].astype(BF16),
            "dww": conv_dw_w[l], "dwb": conv_dw_b[l].reshape(1, -1),
            "lng": conv_ln_g[l].reshape(1, -1), "lnb": conv_ln_b[l].reshape(1, -1), "scw": sc_conv_w[l],
        }
        g0, g1, g2 = (norm_g[l, s].reshape(1, d) for s in range(3))
        qg = q_norm_g[l].reshape(HEAD_DIM, 1)
        kg = k_norm_g[l].reshape(HEAD_DIM, 1)

        xs = _ffn_call(xs, mod_x[0:3], g0, w13[l, 0], w2[l, 0], fg, final=False, tm_target=1024)
        cs = _ffn_call(cs, mod_c[0:3], g0, w13[l, 0], w2[l, 0], fg, final=False, tm_target=1024)

        q_x, k_x, v_x, f_x, h_x, s_x, gb_x = _inproj_call(
            xs, mod_x[3:5], g1, wqkv_t[l], wrest[l], qg, kg, cos_x, sin_x, tm_target=512)
        q_c, k_c, v_c, f_c, h_c, s_c, gb_c = _inproj_call(
            cs, mod_c[3:5], g1, wqkv_t[l], wrest[l], qg, kg, cos_c, sin_c, tm_target=512)

        k_all = jnp.concatenate([k_x, k_c], axis=0)
        v_all = jnp.concatenate([v_x, v_c], axis=1)
        attn_x = _attn_call(q_x, k_all, v_all, tq_target=1024, tk_target=640)
        fn_x = _fourier_call(f_x, fc_x)
        xs = _merge_call(xs, mod_x[3:6], g1, attn_x, fn_x, h_x, s_x, gb_x, lw, tm_target=512)

        if not last:
            attn_c = _attn_call(q_c, k_c, v_c, tq_target=1024, tk_target=640)
            fn_c = _fourier_call(f_c, fc_c)
            cs = _merge_call(cs, mod_c[3:6], g1, attn_c, fn_c, h_c, s_c, gb_c, lw, tm_target=512)

        xs = _ffn_call(xs, mod_x[6:9], g2, w13[l, 1], w2[l, 1], fg, final=last, tm_target=1024)
        if not last:
            cs = _ffn_call(cs, mod_c[6:9], g2, w13[l, 1], w2[l, 1], fg, final=False, tm_target=1024)

    return xs[None]
```

```python
import functools
import math

import numpy as np
import jax
import jax.numpy as jnp
from jax import lax
from jax.experimental import pallas as pl
from jax.experimental.pallas import tpu as pltpu

F32 = jnp.float32
BF16 = jnp.bfloat16

D_MODEL = 1024
GRID_W = 64
N_HEADS = 8
N_KV_HEADS = 2
HEAD_DIM = 64
Q_GROUP = N_HEADS // N_KV_HEADS
Q_W = N_HEADS * HEAD_DIM
KV_W = N_KV_HEADS * HEAD_DIM
ROPE_THETA = 10000.0
FNET_GROUPS = 4
FNET_GROUP_DIM = 64
FNET_WIDTH = FNET_GROUPS * FNET_GROUP_DIM
CONV_WIDTH = 256
CONV_KERNEL = 31
SC_WIDTH = 256
SC_KERNEL = 3
N_BRANCHES = 4
D_FF = 2816
N_MOD = 9
EPS = 1e-6
QKV_W = Q_W + 2 * KV_W
REST_W = FNET_WIDTH + 2 * CONV_WIDTH + 3 * SC_WIDTH
V_ROWS = HEAD_DIM + 16
Q_SCALE = HEAD_DIM ** -0.5 * math.log2(math.e)
SAFE_SCORE_BOUND = 40.0
HALO = 16
FF_CHUNK = 256
V7X_VMEM_LIMIT_BYTES = 56 * 1024 * 1024


def _tile(n, target):
    if n <= target:
        return n
    t = (target // 128) * 128
    while t >= 128:
        if n % t == 0:
            return t
        t -= 128
    return n


def _params(sem):
    return pltpu.CompilerParams(dimension_semantics=sem, vmem_limit_bytes=V7X_VMEM_LIMIT_BYTES)


def _dot(a, b):
    return jnp.dot(a, b, preferred_element_type=F32)


def _dot_nt(a, b):
    return lax.dot_general(a, b, (((1,), (1,)), ((), ())), preferred_element_type=F32)


def _modnorm(x, g, shift, scale):
    y = x * lax.rsqrt(jnp.mean(x * x, axis=-1, keepdims=True) + EPS)
    return (y * g) * (1.0 + scale) + shift


def _silu(x):
    return x * jax.nn.sigmoid(x)


def _mod_kernel(c_ref, w_ref, b_ref, o_ref):
    a = _silu(c_ref[...])
    o_ref[0] = _dot(a.astype(BF16), w_ref[0].astype(BF16)) + b_ref[0]


def _mod_call(cc, w_ada, b_ada):
    depth, d, n = w_ada.shape
    tn = 1152
    return pl.pallas_call(
        _mod_kernel,
        grid=(depth, n // tn),
        in_specs=[
            pl.BlockSpec((8, d), lambda l, j: (0, 0)),
            pl.BlockSpec((1, d, tn), lambda l, j: (l, 0, j)),
            pl.BlockSpec((1, 1, tn), lambda l, j: (l, 0, j)),
        ],
        out_specs=pl.BlockSpec((1, 8, tn), lambda l, j: (l, 0, j)),
        out_shape=jax.ShapeDtypeStruct((depth, 8, n), F32),
        compiler_params=_params(("parallel", "parallel")),
        name="adaln_mod",
    )(cc, w_ada, b_ada.reshape(depth, 1, n))


def _ffn_kernel(x_ref, mod_ref, g_ref, w1_ref, w3_ref, w2_ref, fg_ref, o_ref, h_ref, acc_ref, *, final):
    j = pl.program_id(1)

    @pl.when(j == 0)
    def _():
        h = _modnorm(x_ref[...], g_ref[...], mod_ref[0:1, :], mod_ref[1:2, :])
        h_ref[...] = h.astype(BF16)
        acc_ref[...] = jnp.zeros_like(acc_ref)

    h = h_ref[...]
    a = _dot(h, w1_ref[...])
    b = _dot(h, w3_ref[...])
    acc_ref[...] += _dot((_silu(a) * b).astype(BF16), w2_ref[...])

    @pl.when(j == pl.num_programs(1) - 1)
    def _():
        y = x_ref[...] + (0.5 * mod_ref[2:3, :]) * acc_ref[...]
        if final:
            y = (y * lax.rsqrt(jnp.mean(y * y, axis=-1, keepdims=True) + EPS)) * fg_ref[...]
        o_ref[...] = y


def _ffn_call(x, mod3, g, w13, w2, fg, *, final, tm_target):
    n, d = x.shape
    tm = _tile(n, tm_target)
    nf = D_FF // FF_CHUNK
    return pl.pallas_call(
        functools.partial(_ffn_kernel, final=final),
        grid=(n // tm, nf),
        in_specs=[
            pl.BlockSpec((tm, d), lambda i, j: (i, 0)),
            pl.BlockSpec((3, d), lambda i, j: (0, 0)),
            pl.BlockSpec((1, d), lambda i, j: (0, 0)),
            pl.BlockSpec((d, FF_CHUNK), lambda i, j: (0, j)),
            pl.BlockSpec((d, FF_CHUNK), lambda i, j: (0, j + nf)),
            pl.BlockSpec((FF_CHUNK, d), lambda i, j: (j, 0)),
            pl.BlockSpec((1, d), lambda i, j: (0, 0)),
        ],
        out_specs=pl.BlockSpec((tm, d), lambda i, j: (i, 0)),
        out_shape=jax.ShapeDtypeStruct((n, d), F32),
        scratch_shapes=[pltpu.VMEM((tm, d), BF16), pltpu.VMEM((tm, d), F32)],
        compiler_params=_params(("parallel", "arbitrary")),
        name="ffn_half",
    )(x, mod3, g, w13, w13, w2, fg)


def _head_norm_rope(t, gain, cos, sin, scale):
    ms = jnp.sum(t * t, axis=0, keepdims=True) * (1.0 / HEAD_DIM)
    t = (t * lax.rsqrt(ms + EPS)) * gain
    quarter = HEAD_DIM // 4
    parts = []
    for axis in range(2):
        x1 = t[2 * axis * quarter:(2 * axis + 1) * quarter]
        x2 = t[(2 * axis + 1) * quarter:(2 * axis + 2) * quarter]
        c = cos[axis * quarter:(axis + 1) * quarter]
        s = sin[axis * quarter:(axis + 1) * quarter]
        parts += [x1 * c - x2 * s, x1 * s + x2 * c]
    out = jnp.concatenate(parts, axis=0)
    return out * scale if scale != 1.0 else out


def _inproj_kernel(x_ref, mod_ref, g_ref, wqkv_ref, wrest_ref, qg_ref, kg_ref, cos_ref, sin_ref,
                   q_ref, k_ref, v_ref, f_ref, h_ref, s_ref, gb_ref):
    u = _modnorm(x_ref[...], g_ref[...], mod_ref[0:1, :], mod_ref[1:2, :]).astype(BF16)
    qkv = _dot_nt(wqkv_ref[...], u)
    cos = cos_ref[...]
    sin = sin_ref[...]
    zeros = jnp.zeros((HEAD_DIM, qkv.shape[1]), F32)
    for h in range(N_HEADS):
        t = _head_norm_rope(qkv[h * HEAD_DIM:(h + 1) * HEAD_DIM], qg_ref[...], cos, sin, Q_SCALE)
        blk = [t, zeros] if h // Q_GROUP == 0 else [zeros, t]
        q_ref[h * KV_W:(h + 1) * KV_W, :] = jnp.concatenate(blk, axis=0).astype(BF16)
    kt = [_head_norm_rope(qkv[Q_W + h * HEAD_DIM:Q_W + (h + 1) * HEAD_DIM], kg_ref[...], cos, sin, 1.0)
          for h in range(N_KV_HEADS)]
    k_ref[...] = jnp.concatenate(kt, axis=0).T.astype(BF16)
    ones = jnp.ones((V_ROWS - HEAD_DIM, qkv.shape[1]), F32)
    vt = []
    for h in range(N_KV_HEADS):
        vt += [qkv[Q_W + KV_W + h * HEAD_DIM:Q_W + KV_W + (h + 1) * HEAD_DIM], ones]
    v_ref[...] = jnp.concatenate(vt, axis=0).astype(BF16)

    r = _dot(u, wrest_ref[...])
    w = FNET_WIDTH
    f_ref[...] = r[:, 0:w]
    h_ref[...] = r[:, w:2 * w] * jax.nn.sigmoid(r[:, 2 * w:3 * w])
    gb_ref[...] = r[:, 3 * w:4 * w]
    s_ref[...] = r[:, 4 * w:5 * w] * r[:, 5 * w:6 * w]


def _inproj_call(x, mod2, g, wqkv_t, wrest, qg, kg, cos_t, sin_t, *, tm_target):
    n, d = x.shape
    tm = _tile(n, tm_target)
    row = lambda i: (i, 0)
    col = lambda i: (0, i)
    fix = lambda i: (0, 0)
    w = FNET_WIDTH
    return pl.pallas_call(
        _inproj_kernel,
        grid=(n // tm,),
        in_specs=[
            pl.BlockSpec((tm, d), row),
            pl.BlockSpec((2, d), fix),
            pl.BlockSpec((1, d), fix),
            pl.BlockSpec((QKV_W, d), fix),
            pl.BlockSpec((d, REST_W), fix),
            pl.BlockSpec((HEAD_DIM, 1), fix),
            pl.BlockSpec((HEAD_DIM, 1), fix),
            pl.BlockSpec((HEAD_DIM // 2, tm), col),
            pl.BlockSpec((HEAD_DIM // 2, tm), col),
        ],
        out_specs=[
            pl.BlockSpec((N_HEADS * KV_W, tm), col),
            pl.BlockSpec((tm, KV_W), row),
            pl.BlockSpec((N_KV_HEADS * V_ROWS, tm), col),
            pl.BlockSpec((tm, w), row),
            pl.BlockSpec((tm, w), row),
            pl.BlockSpec((tm, w), row),
            pl.BlockSpec((tm, w), row),
        ],
        out_shape=[
            jax.ShapeDtypeStruct((N_HEADS * KV_W, n), BF16),
            jax.ShapeDtypeStruct((n, KV_W), BF16),
            jax.ShapeDtypeStruct((N_KV_HEADS * V_ROWS, n), BF16),
            jax.ShapeDtypeStruct((n, w), F32),
            jax.ShapeDtypeStruct((n, w), F32),
            jax.ShapeDtypeStruct((n, w), F32),
            jax.ShapeDtypeStruct((n, w), F32),
        ],
        compiler_params=_params(("parallel",)),
        name="in_proj",
    )(x, mod2, g, wqkv_t, wrest, qg, kg, cos_t, sin_t)


def _attn_kernel(safe_ref, q_ref, k_ref, v_ref, o_ref, m_ref, acc_ref):
    j = pl.program_id(2)

    @pl.when(j == 0)
    def _():
        m_ref[...] = jnp.full_like(m_ref, -jnp.inf)
        acc_ref[...] = jnp.zeros_like(acc_ref)

    def scores(h):
        return _dot(k_ref[...], q_ref[h * KV_W:(h + 1) * KV_W, :])

    @pl.when(safe_ref[0] != 0)
    def _():
        for h in range(Q_GROUP):
            rows = slice(h * V_ROWS, (h + 1) * V_ROWS)
            acc_ref[rows, :] += _dot(v_ref[...], jnp.exp2(scores(h)).astype(BF16))

    @pl.when(safe_ref[0] == 0)
    def _():
        for h in range(Q_GROUP):
            s = scores(h)
            m_old = m_ref[h:h + 1, :]
            m_new = jnp.maximum(m_old, jnp.max(s, axis=0, keepdims=True))
            rows = slice(h * V_ROWS, (h + 1) * V_ROWS)
            acc_ref[rows, :] = (jnp.exp2(m_old - m_new) * acc_ref[rows, :]
                                + _dot(v_ref[...], jnp.exp2(s - m_new).astype(BF16)))
            m_ref[h:h + 1, :] = m_new

    @pl.when(j == pl.num_programs(2) - 1)
    def _():
        outs = []
        for h in range(Q_GROUP):
            r0 = h * V_ROWS
            outs.append(acc_ref[r0:r0 + HEAD_DIM, :] / acc_ref[r0 + HEAD_DIM:r0 + HEAD_DIM + 1, :])
        o_ref[...] = jnp.concatenate(outs, axis=0).T.astype(BF16)


def _attn_call(safe, q_t, k, v_t, *, tq_target, tk_target):
    nq = q_t.shape[1]
    nk = k.shape[0]
    tq = _tile(nq, tq_target)
    tk = _tile(nk, tk_target)
    gw = Q_GROUP * HEAD_DIM
    return pl.pallas_call(
        _attn_kernel,
        grid=(N_KV_HEADS, nq // tq, nk // tk),
        in_specs=[
            pl.BlockSpec(memory_space=pltpu.SMEM),
            pl.BlockSpec((Q_GROUP * KV_W, tq), lambda g, i, j: (g, i)),
            pl.BlockSpec((tk, KV_W), lambda g, i, j: (j, 0)),
            pl.BlockSpec((V_ROWS, tk), lambda g, i, j: (g, j)),
        ],
        out_specs=pl.BlockSpec((tq, gw), lambda g, i, j: (i, g)),
        out_shape=jax.ShapeDtypeStruct((nq, Q_W), BF16),
        scratch_shapes=[pltpu.VMEM((8, tq), F32), pltpu.VMEM((Q_GROUP * V_ROWS, tq), F32)],
        compiler_params=_params(("parallel", "parallel", "arbitrary")),
        name="attention",
    )(safe, q_t, k, v_t)


def _split_len(n):
    b = 1 << (int(math.log2(n)) // 2)
    return n // b, b


def _channel_dft(scale):
    c = np.arange(FNET_GROUP_DIM)
    ang = 2.0 * np.pi * np.outer(c, c) / FNET_GROUP_DIM
    eye = np.eye(FNET_GROUPS)
    return np.concatenate([np.kron(eye, np.cos(ang)), -np.kron(eye, np.sin(ang))], axis=1) * scale


def _stage1_tables(n):
    a, b = _split_len(n)
    c = np.arange(a)[None, :, None]
    pos = b * np.arange(a)[None, None, :] + np.arange(b)[:, None, None]
    ang = 2.0 * np.pi * ((c * pos) % n) / n
    co, si = np.cos(ang), np.sin(ang)
    return np.concatenate([np.concatenate([co, si], axis=2), np.concatenate([-si, co], axis=2)], axis=1)


def _stage2_table(n):
    a, b = _split_len(n)
    ang = 2.0 * np.pi * np.outer(np.arange(b), np.arange(b)) / b
    eye = np.eye(8)
    return np.concatenate([np.kron(np.cos(ang), eye), np.kron(np.sin(ang), eye)], axis=1)


def _dft_tables(n):
    ang = 2.0 * np.pi * np.outer(np.arange(n), np.arange(n)) / n
    return np.concatenate([np.cos(ang), np.sin(ang)], axis=1)


def _fft1_kernel(f_ref, t_ref, d_ref, o_ref, *, nb):
    w = FNET_WIDTH
    for b in range(nb):
        z = _dot(f_ref[:, b * w:(b + 1) * w].astype(BF16), d_ref[...])
        zz = jnp.concatenate([z[:, :w], z[:, w:]], axis=0).astype(BF16)
        o_ref[b] = _dot(t_ref[b], zz)


def _fft2_kernel(re_ref, im_ref, t_ref, o_ref):
    nb, _, w = re_ref.shape
    st = jnp.concatenate([re_ref[...].reshape(nb * 8, w), im_ref[...].reshape(nb * 8, w)], axis=0)
    o_ref[...] = _dot(t_ref[...], st.astype(BF16)).reshape(nb, 8, w)


def _fft_small_kernel(f_ref, t_ref, d_ref, o_ref):
    w = FNET_WIDTH
    z = _dot(f_ref[...].astype(BF16), d_ref[...])
    zz = jnp.concatenate([z[:, :w], z[:, w:]], axis=0).astype(BF16)
    o_ref[...] = _dot(t_ref[...], zz)


def _fourier_call(f, consts):
    n, w = f.shape
    if "t1" not in consts:
        return pl.pallas_call(
            _fft_small_kernel,
            out_shape=jax.ShapeDtypeStruct((n, w), F32),
            compiler_params=pltpu.CompilerParams(vmem_limit_bytes=V7X_VMEM_LIMIT_BYTES),
            name="fourier_small",
        )(f, consts["t"], consts["d"])
    a, b = _split_len(n)
    nb = 8
    mid = pl.pallas_call(
        functools.partial(_fft1_kernel, nb=nb),
        grid=(b // nb,),
        in_specs=[
            pl.BlockSpec((a, nb * w), lambda i: (0, i)),
            pl.BlockSpec((nb, 2 * a, 2 * a), lambda i: (i, 0, 0)),
            pl.BlockSpec((w, 2 * w), lambda i: (0, 0)),
        ],
        out_specs=pl.BlockSpec((nb, 2 * a, w), lambda i: (i, 0, 0)),
        out_shape=jax.ShapeDtypeStruct((b, 2 * a, w), F32),
        compiler_params=_params(("parallel",)),
        name="fourier_stage1",
    )(f.reshape(a, b * w), consts["t1"], consts["d"])
    mid = mid.reshape(b, 2, a, w)
    out = pl.pallas_call(
        _fft2_kernel,
        grid=(a // 8,),
        in_specs=[
            pl.BlockSpec((b, None, 8, w), lambda i: (0, 0, i, 0)),
            pl.BlockSpec((b, None, 8, w), lambda i: (0, 1, i, 0)),
            pl.BlockSpec((8 * b, 16 * b), lambda i: (0, 0)),
        ],
        out_specs=pl.BlockSpec((b, 8, w), lambda i: (0, i, 0)),
        out_shape=jax.ShapeDtypeStruct((b, a, w), F32),
        compiler_params=_params(("parallel",)),
        name="fourier_stage2",
    )(mid, mid, consts["t2"])
    return out.reshape(n, w)


def _fourier_consts(n):
    scale = 1.0 / math.sqrt(n * FNET_GROUP_DIM)
    d = jnp.asarray(_channel_dft(scale), BF16)
    if n <= 512:
        return {"d": d, "t": jnp.asarray(_dft_tables(n), BF16)}
    return {"d": d, "t1": jnp.asarray(_stage1_tables(n), BF16), "t2": jnp.asarray(_stage2_table(n), BF16)}


def _fill_window(win_ref, prev_ref, cur_ref, next_ref, tm):
    i = pl.program_id(0)
    last = pl.num_programs(0) - 1
    win_ref[0:HALO, :] = jnp.where(i > 0, prev_ref[tm - HALO:tm, :], 0.0)
    win_ref[HALO:HALO + tm, :] = cur_ref[...]
    win_ref[HALO + tm:HALO + tm + HALO, :] = jnp.where(i < last, next_ref[0:HALO, :], 0.0)


def _depthwise(win_ref, w_ref, taps, tm):
    pad = (taps - 1) // 2
    acc = None
    for t in range(taps):
        term = win_ref[pl.ds(HALO - pad + t, tm), :] * w_ref[t:t + 1, :]
        acc = term if acc is None else acc + term
    return acc


def _merge_kernel(x_ref, mod_ref, g_ref, attn_ref, fn_ref, hp_ref, hc_ref, hn_ref, sp_ref, sc_ref, sn_ref,
                  gb_ref, wg_ref, bg_ref, wa_ref, wf_ref, wc_ref, ws_ref, wo_ref,
                  dww_ref, dwb_ref, lng_ref, lnb_ref, scw_ref, o_ref, hwin_ref, swin_ref):
    tm, d = x_ref.shape
    x = x_ref[...]
    u = _modnorm(x, g_ref[...], mod_ref[0:1, :], mod_ref[1:2, :]).astype(BF16)

    _fill_window(hwin_ref, hp_ref, hc_ref, hn_ref, tm)
    _fill_window(swin_ref, sp_ref, sc_ref, sn_ref, tm)
    c = _depthwise(hwin_ref, dww_ref, CONV_KERNEL, tm) + dwb_ref[...]
    mu = jnp.mean(c, axis=-1, keepdims=True)
    cc = c - mu
    var = jnp.mean(cc * cc, axis=-1, keepdims=True)
    conf = _silu((cc * lax.rsqrt(var + EPS)) * lng_ref[...] + lnb_ref[...])
    short = gb_ref[...] * _depthwise(swin_ref, scw_ref, SC_KERNEL, tm)

    branches = (
        (attn_ref[...], wa_ref),
        (fn_ref[...].astype(BF16), wf_ref),
        (conf.astype(BF16), wc_ref),
        (short.astype(BF16), ws_ref),
    )
    merged = None
    for i, (val, w_ref) in enumerate(branches):
        cols = slice(i * d, (i + 1) * d)
        gate = jax.nn.sigmoid(_dot(u, wg_ref[:, cols]) + bg_ref[:, cols])
        term = gate * _dot(val, w_ref[...])
        merged = term if merged is None else merged + term
    o_ref[...] = x + mod_ref[2:3, :] * _dot(merged.astype(BF16), wo_ref[...])


def _merge_call(x, mod3, g, attn, fn, hglu, sprod, gb, lw, *, tm_target):
    n, d = x.shape
    tm = _tile(n, tm_target)
    nt = n // tm
    row = lambda i: (i, 0)
    prev = lambda i: (jnp.maximum(i - 1, 0), 0)
    nxt = lambda i: (jnp.minimum(i + 1, nt - 1), 0)
    fix = lambda i: (0, 0)
    w = CONV_WIDTH

    def whole(arr):
        return pl.BlockSpec(arr.shape, fix, pipeline_mode=pl.Buffered(1))

    weights = (lw["wg"], lw["bg"], lw["wa"], lw["wf"], lw["wc"], lw["ws"], lw["wo"],
               lw["dww"], lw["dwb"], lw["lng"], lw["lnb"], lw["scw"])
    return pl.pallas_call(
        _merge_kernel,
        grid=(nt,),
        in_specs=[
            pl.BlockSpec((tm, d), row),
            pl.BlockSpec((3, d), fix),
            pl.BlockSpec((1, d), fix),
            pl.BlockSpec((tm, Q_W), row),
            pl.BlockSpec((tm, FNET_WIDTH), row),
            pl.BlockSpec((tm, w), prev), pl.BlockSpec((tm, w), row), pl.BlockSpec((tm, w), nxt),
            pl.BlockSpec((tm, w), prev), pl.BlockSpec((tm, w), row), pl.BlockSpec((tm, w), nxt),
            pl.BlockSpec((tm, w), row),
        ] + [whole(a) for a in weights],
        out_specs=pl.BlockSpec((tm, d), row),
        out_shape=jax.ShapeDtypeStruct((n, d), F32),
        scratch_shapes=[pltpu.VMEM((tm + 2 * HALO, w), F32), pltpu.VMEM((tm + 2 * HALO, w), F32)],
        compiler_params=_params(("arbitrary",)),
        name="mix_merge",
    )(x, mod3, g, attn, fn, hglu, hglu, hglu, sprod, sprod, sprod, gb, *weights)


def _rope_tables_t(n_tokens):
    n_rows = n_tokens // GRID_W
    row = jnp.broadcast_to(jnp.arange(n_rows)[:, None], (n_rows, GRID_W)).reshape(-1)
    col = jnp.broadcast_to(jnp.arange(GRID_W)[None, :], (n_rows, GRID_W)).reshape(-1)
    axis_dim = HEAD_DIM // 2
    inv_freq = ROPE_THETA ** (-jnp.arange(0, axis_dim, 2, dtype=F32) / axis_dim)
    pos = jnp.stack([row, col], axis=-1).astype(F32)
    ang = pos[:, :, None] * inv_freq
    ang_t = ang.reshape(n_tokens, axis_dim).T
    return jnp.cos(ang_t), jnp.sin(ang_t)


def kernel(x, c, ctx, c_ctx, w_ada, b_ada, norm_g, ffn_w13, ffn_w2, w_in, b_gate, q_norm_g, k_norm_g,
           w_attn_out, w_fnet, conv_dw_w, conv_dw_b, conv_ln_g, conv_ln_b, w_conv_out, sc_conv_w,
           w_sc_out, w_o, final_norm_g):
    batch, seq, d = x.shape
    assert batch == 1 and d == D_MODEL
    n_ctx = ctx.shape[1]
    depth = w_ada.shape[0]

    xs = x[0]
    cs = ctx[0]
    cos_x, sin_x = _rope_tables_t(seq)
    cos_c = jnp.ones((HEAD_DIM // 2, n_ctx), F32)
    sin_c = jnp.zeros((HEAD_DIM // 2, n_ctx), F32)
    fc_x = _fourier_consts(seq)
    fc_c = _fourier_consts(n_ctx)

    cc = jnp.zeros((8, d), F32).at[0].set(c[0]).at[1].set(c_ctx)
    mod = _mod_call(cc, w_ada, b_ada)
    mod = mod.reshape(depth, 8, N_MOD, d)

    w13 = ffn_w13.astype(BF16)
    w2 = ffn_w2.astype(BF16)
    wqkv_t = jnp.swapaxes(w_in[:, :, :QKV_W], 1, 2).astype(BF16)
    wrest = w_in[:, :, QKV_W:QKV_W + REST_W].astype(BF16)
    wg = w_in[:, :, QKV_W + REST_W:].astype(BF16)
    fg = final_norm_g.reshape(1, d)

    for l in range(depth):
        last = l == depth - 1
        mod_x, mod_c = mod[l, 0], mod[l, 1]
        lw = {
            "wg": wg[l], "bg": b_gate[l].reshape(1, -1),
            "wa": w_attn_out[l].astype(BF16), "wf": w_fnet[l].astype(BF16),
            "wc": w_conv_out[l].astype(BF16), "ws": w_sc_out[l].astype(BF16), "wo": w_o[l].astype(BF16),
            "dww": conv_dw_w[l], "dwb": conv_dw_b[l].reshape(1, -1),
            "lng": conv_ln_g[l].reshape(1, -1), "lnb": conv_ln_b[l].reshape(1, -1), "scw": sc_conv_w[l],
        }
        g0, g1, g2 = (norm_g[l, s].reshape(1, d) for s in range(3))
        qg = q_norm_g[l].reshape(HEAD_DIM, 1)
        kg = k_norm_g[l].reshape(HEAD_DIM, 1)

        xs = _ffn_call(xs, mod_x[0:3], g0, w13[l, 0], w2[l, 0], fg, final=False, tm_target=1024)
        cs = _ffn_call(cs, mod_c[0:3], g0, w13[l, 0], w2[l, 0], fg, final=False, tm_target=1024)

        q_x, k_x, v_x, f_x, h_x, s_x, gb_x = _inproj_call(
            xs, mod_x[3:5], g1, wqkv_t[l], wrest[l], qg, kg, cos_x, sin_x, tm_target=512)
        q_c, k_c, v_c, f_c, h_c, s_c, gb_c = _inproj_call(
            cs, mod_c[3:5], g1, wqkv_t[l], wrest[l], qg, kg, cos_c, sin_c, tm_target=512)

        k_all = jnp.concatenate([k_x, k_c], axis=0)
        v_all = jnp.concatenate([v_x, v_c], axis=1)
        bound = 8.0 * jnp.max(jnp.abs(q_norm_g[l])) * jnp.max(jnp.abs(k_norm_g[l]))
        safe = (bound <= SAFE_SCORE_BOUND).astype(jnp.int32).reshape(1)
        attn_x = _attn_call(safe, q_x, k_all, v_all, tq_target=1024, tk_target=1280)
        fn_x = _fourier_call(f_x, fc_x)
        xs = _merge_call(xs, mod_x[3:6], g1, attn_x, fn_x, h_x, s_x, gb_x, lw, tm_target=512)

        if not last:
            attn_c = _attn_call(safe, q_c, k_c, v_c, tq_target=1024, tk_target=1280)
            fn_c = _fourier_call(f_c, fc_c)
            cs = _merge_call(cs, mod_c[3:6], g1, attn_c, fn_c, h_c, s_c, gb_c, lw, tm_target=512)

        xs = _ffn_call(xs, mod_x[6:9], g2, w13[l, 1], w2[l, 1], fg, final=last, tm_target=1024)
        if not last:
            cs = _ffn_call(cs, mod_c[6:9], g2, w13[l, 1], w2[l, 1], fg, final=False, tm_target=1024)

    return xs[None]
```

```python
import functools
import math

import numpy as np
import jax
import jax.numpy as jnp
from jax import lax
from jax.experimental import pallas as pl
from jax.experimental.pallas import tpu as pltpu

F32 = jnp.float32
BF16 = jnp.bfloat16

D_MODEL = 1024
GRID_W = 64
N_HEADS = 8
N_KV_HEADS = 2
HEAD_DIM = 64
Q_GROUP = N_HEADS // N_KV_HEADS
Q_W = N_HEADS * HEAD_DIM
KV_W = N_KV_HEADS * HEAD_DIM
ROPE_THETA = 10000.0
FNET_GROUPS = 4
FNET_GROUP_DIM = 64
FNET_WIDTH = FNET_GROUPS * FNET_GROUP_DIM
CONV_WIDTH = 256
CONV_KERNEL = 31
SC_WIDTH = 256
SC_KERNEL = 3
N_BRANCHES = 4
D_FF = 2816
N_MOD = 9
EPS = 1e-6
QKV_W = Q_W + 2 * KV_W
REST_W = FNET_WIDTH + 2 * CONV_WIDTH + 3 * SC_WIDTH
V_ROWS = HEAD_DIM + 16
Q_SCALE = HEAD_DIM ** -0.5 * math.log2(math.e)
SAFE_SCORE_BOUND = 40.0
HALO = 16
FF_CHUNK = 256
MERGE_SPLIT = 2
V7X_VMEM_LIMIT_BYTES = 56 * 1024 * 1024


def _tile(n, target):
    if n <= target:
        return n
    t = (target // 128) * 128
    while t >= 128:
        if n % t == 0:
            return t
        t -= 128
    return n


def _params(sem):
    return pltpu.CompilerParams(dimension_semantics=sem, vmem_limit_bytes=V7X_VMEM_LIMIT_BYTES)


def _dot(a, b):
    return jnp.dot(a, b, preferred_element_type=F32)


def _dot_nt(a, b):
    return lax.dot_general(a, b, (((1,), (1,)), ((), ())), preferred_element_type=F32)


def _modnorm(x, g, shift, scale):
    y = x * lax.rsqrt(jnp.mean(x * x, axis=-1, keepdims=True) + EPS)
    return (y * g) * (1.0 + scale) + shift


def _silu(x):
    return x * jax.nn.sigmoid(x)


def _mod_kernel(c_ref, w_ref, b_ref, o_ref):
    a = _silu(c_ref[...])
    o_ref[0] = _dot(a.astype(BF16), w_ref[0].astype(BF16)) + b_ref[0]


def _mod_call(cc, w_ada, b_ada):
    depth, d, n = w_ada.shape
    tn = 1152
    return pl.pallas_call(
        _mod_kernel,
        grid=(depth, n // tn),
        in_specs=[
            pl.BlockSpec((8, d), lambda l, j: (0, 0)),
            pl.BlockSpec((1, d, tn), lambda l, j: (l, 0, j)),
            pl.BlockSpec((1, 1, tn), lambda l, j: (l, 0, j)),
        ],
        out_specs=pl.BlockSpec((1, 8, tn), lambda l, j: (l, 0, j)),
        out_shape=jax.ShapeDtypeStruct((depth, 8, n), F32),
        compiler_params=_params(("parallel", "parallel")),
        name="adaln_mod",
    )(cc, w_ada, b_ada.reshape(depth, 1, n))


def _ffn_kernel(x_ref, mod_ref, g_ref, w13_ref, w2_ref, fg_ref, o_ref, *, final):
    nf = w2_ref.shape[0]
    x = x_ref[...]
    h = _modnorm(x, g_ref[...], mod_ref[0:1, :], mod_ref[1:2, :]).astype(BF16)
    acc = None
    for c in range(nf):
        a = _dot(h, w13_ref[c])
        b = _dot(h, w13_ref[nf + c])
        t = _dot((_silu(a) * b).astype(BF16), w2_ref[c])
        acc = t if acc is None else acc + t
    y = x + (0.5 * mod_ref[2:3, :]) * acc
    if final:
        y = (y * lax.rsqrt(jnp.mean(y * y, axis=-1, keepdims=True) + EPS)) * fg_ref[...]
    o_ref[...] = y


def _ffn_call(x, mod3, g, w13c, w2c, fg, *, final, tm_target):
    n, d = x.shape
    tm = _tile(n, tm_target)
    fix2 = lambda i: (0, 0)
    fix3 = lambda i: (0, 0, 0)
    return pl.pallas_call(
        functools.partial(_ffn_kernel, final=final),
        grid=(n // tm,),
        in_specs=[
            pl.BlockSpec((tm, d), lambda i: (i, 0)),
            pl.BlockSpec((3, d), fix2),
            pl.BlockSpec((1, d), fix2),
            pl.BlockSpec(w13c.shape, fix3, pipeline_mode=pl.Buffered(1)),
            pl.BlockSpec(w2c.shape, fix3, pipeline_mode=pl.Buffered(1)),
            pl.BlockSpec((1, d), fix2),
        ],
        out_specs=pl.BlockSpec((tm, d), lambda i: (i, 0)),
        out_shape=jax.ShapeDtypeStruct((n, d), F32),
        compiler_params=_params(("parallel",)),
        name="ffn_half",
    )(x, mod3, g, w13c, w2c, fg)


def _head_norm_rope(t, gain, cos, sin, scale):
    ms = jnp.sum(t * t, axis=0, keepdims=True) * (1.0 / HEAD_DIM)
    t = (t * lax.rsqrt(ms + EPS)) * gain
    quarter = HEAD_DIM // 4
    parts = []
    for axis in range(2):
        x1 = t[2 * axis * quarter:(2 * axis + 1) * quarter]
        x2 = t[(2 * axis + 1) * quarter:(2 * axis + 2) * quarter]
        c = cos[axis * quarter:(axis + 1) * quarter]
        s = sin[axis * quarter:(axis + 1) * quarter]
        parts += [x1 * c - x2 * s, x1 * s + x2 * c]
    out = jnp.concatenate(parts, axis=0)
    return out * scale if scale != 1.0 else out


def _inproj_kernel(x_ref, mod_ref, g_ref, wqkv_ref, wrest_ref, qg_ref, kg_ref, cos_ref, sin_ref,
                   q_ref, k_ref, v_ref, f_ref, h_ref, s_ref, gb_ref):
    u = _modnorm(x_ref[...], g_ref[...], mod_ref[0:1, :], mod_ref[1:2, :]).astype(BF16)
    qkv = _dot_nt(wqkv_ref[...], u)
    cos = cos_ref[...]
    sin = sin_ref[...]
    zeros = jnp.zeros((HEAD_DIM, qkv.shape[1]), F32)
    for h in range(N_HEADS):
        t = _head_norm_rope(qkv[h * HEAD_DIM:(h + 1) * HEAD_DIM], qg_ref[...], cos, sin, Q_SCALE)
        blk = [t, zeros] if h // Q_GROUP == 0 else [zeros, t]
        q_ref[h * KV_W:(h + 1) * KV_W, :] = jnp.concatenate(blk, axis=0).astype(BF16)
    kt = [_head_norm_rope(qkv[Q_W + h * HEAD_DIM:Q_W + (h + 1) * HEAD_DIM], kg_ref[...], cos, sin, 1.0)
          for h in range(N_KV_HEADS)]
    k_ref[...] = jnp.concatenate(kt, axis=0).T.astype(BF16)
    ones = jnp.ones((V_ROWS - HEAD_DIM, qkv.shape[1]), F32)
    vt = []
    for h in range(N_KV_HEADS):
        vt += [qkv[Q_W + KV_W + h * HEAD_DIM:Q_W + KV_W + (h + 1) * HEAD_DIM], ones]
    v_ref[...] = jnp.concatenate(vt, axis=0).astype(BF16)

    r = _dot(u, wrest_ref[...])
    w = FNET_WIDTH
    f_ref[...] = r[:, 0:w]
    h_ref[...] = r[:, w:2 * w] * jax.nn.sigmoid(r[:, 2 * w:3 * w])
    gb_ref[...] = r[:, 3 * w:4 * w]
    s_ref[...] = r[:, 4 * w:5 * w] * r[:, 5 * w:6 * w]


def _inproj_call(x, mod2, g, wqkv_t, wrest, qg, kg, cos_t, sin_t, *, tm_target):
    n, d = x.shape
    tm = _tile(n, tm_target)
    row = lambda i: (i, 0)
    col = lambda i: (0, i)
    fix = lambda i: (0, 0)
    w = FNET_WIDTH
    return pl.pallas_call(
        _inproj_kernel,
        grid=(n // tm,),
        in_specs=[
            pl.BlockSpec((tm, d), row),
            pl.BlockSpec((2, d), fix),
            pl.BlockSpec((1, d), fix),
            pl.BlockSpec((QKV_W, d), fix),
            pl.BlockSpec((d, REST_W), fix),
            pl.BlockSpec((HEAD_DIM, 1), fix),
            pl.BlockSpec((HEAD_DIM, 1), fix),
            pl.BlockSpec((HEAD_DIM // 2, tm), col),
            pl.BlockSpec((HEAD_DIM // 2, tm), col),
        ],
        out_specs=[
            pl.BlockSpec((N_HEADS * KV_W, tm), col),
            pl.BlockSpec((tm, KV_W), row),
            pl.BlockSpec((N_KV_HEADS * V_ROWS, tm), col),
            pl.BlockSpec((tm, w), row),
            pl.BlockSpec((tm, w), row),
            pl.BlockSpec((tm, w), row),
            pl.BlockSpec((tm, w), row),
        ],
        out_shape=[
            jax.ShapeDtypeStruct((N_HEADS * KV_W, n), BF16),
            jax.ShapeDtypeStruct((n, KV_W), BF16),
            jax.ShapeDtypeStruct((N_KV_HEADS * V_ROWS, n), BF16),
            jax.ShapeDtypeStruct((n, w), F32),
            jax.ShapeDtypeStruct((n, w), F32),
            jax.ShapeDtypeStruct((n, w), F32),
            jax.ShapeDtypeStruct((n, w), F32),
        ],
        compiler_params=_params(("parallel",)),
        name="in_proj",
    )(x, mod2, g, wqkv_t, wrest, qg, kg, cos_t, sin_t)


def _attn_kernel(safe_ref, q_ref, k_ref, v_ref, o_ref, m_ref, acc_ref):
    j = pl.program_id(2)

    @pl.when(j == 0)
    def _():
        m_ref[...] = jnp.full_like(m_ref, -jnp.inf)
        acc_ref[...] = jnp.zeros_like(acc_ref)

    def scores(h):
        return _dot(k_ref[...], q_ref[h * KV_W:(h + 1) * KV_W, :])

    @pl.when(safe_ref[0] != 0)
    def _():
        for h in range(Q_GROUP):
            rows = slice(h * V_ROWS, (h + 1) * V_ROWS)
            acc_ref[rows, :] += _dot(v_ref[...], jnp.exp2(scores(h)).astype(BF16))

    @pl.when(safe_ref[0] == 0)
    def _():
        for h in range(Q_GROUP):
            s = scores(h)
            m_old = m_ref[h:h + 1, :]
            m_new = jnp.maximum(m_old, jnp.max(s, axis=0, keepdims=True))
            rows = slice(h * V_ROWS, (h + 1) * V_ROWS)
            acc_ref[rows, :] = (jnp.exp2(m_old - m_new) * acc_ref[rows, :]
                                + _dot(v_ref[...], jnp.exp2(s - m_new).astype(BF16)))
            m_ref[h:h + 1, :] = m_new

    @pl.when(j == pl.num_programs(2) - 1)
    def _():
        outs = []
        for h in range(Q_GROUP):
            r0 = h * V_ROWS
            outs.append(acc_ref[r0:r0 + HEAD_DIM, :] / acc_ref[r0 + HEAD_DIM:r0 + HEAD_DIM + 1, :])
        o_ref[...] = jnp.concatenate(outs, axis=0).T.astype(BF16)


def _attn_call(safe, q_t, k, v_t, *, tq_target, tk_target):
    nq = q_t.shape[1]
    nk = k.shape[0]
    tq = _tile(nq, tq_target)
    tk = _tile(nk, tk_target)
    gw = Q_GROUP * HEAD_DIM
    return pl.pallas_call(
        _attn_kernel,
        grid=(N_KV_HEADS, nq // tq, nk // tk),
        in_specs=[
            pl.BlockSpec(memory_space=pltpu.SMEM),
            pl.BlockSpec((Q_GROUP * KV_W, tq), lambda g, i, j: (g, i)),
            pl.BlockSpec((tk, KV_W), lambda g, i, j: (j, 0)),
            pl.BlockSpec((V_ROWS, tk), lambda g, i, j: (g, j)),
        ],
        out_specs=pl.BlockSpec((tq, gw), lambda g, i, j: (i, g)),
        out_shape=jax.ShapeDtypeStruct((nq, Q_W), BF16),
        scratch_shapes=[pltpu.VMEM((8, tq), F32), pltpu.VMEM((Q_GROUP * V_ROWS, tq), F32)],
        compiler_params=_params(("parallel", "parallel", "arbitrary")),
        name="attention",
    )(safe, q_t, k, v_t)


def _split_len(n):
    b = 1 << (int(math.log2(n)) // 2)
    return n // b, b


def _channel_dft(scale):
    c = np.arange(FNET_GROUP_DIM)
    ang = 2.0 * np.pi * np.outer(c, c) / FNET_GROUP_DIM
    eye = np.eye(FNET_GROUPS)
    return np.concatenate([np.kron(eye, np.cos(ang)), -np.kron(eye, np.sin(ang))], axis=1) * scale


def _stage1_tables(n):
    a, b = _split_len(n)
    c = np.arange(a)[None, :, None]
    pos = b * np.arange(a)[None, None, :] + np.arange(b)[:, None, None]
    ang = 2.0 * np.pi * ((c * pos) % n) / n
    co, si = np.cos(ang), np.sin(ang)
    return np.concatenate([np.concatenate([co, si], axis=2), np.concatenate([-si, co], axis=2)], axis=1)


def _stage2_table(n):
    a, b = _split_len(n)
    ang = 2.0 * np.pi * np.outer(np.arange(b), np.arange(b)) / b
    eye = np.eye(8)
    return np.concatenate([np.kron(np.cos(ang), eye), np.kron(np.sin(ang), eye)], axis=1)


def _dft_tables(n):
    ang = 2.0 * np.pi * np.outer(np.arange(n), np.arange(n)) / n
    return np.concatenate([np.cos(ang), np.sin(ang)], axis=1)


def _fft1_kernel(f_ref, t_ref, d_ref, o_ref, *, nb):
    w = FNET_WIDTH
    for b in range(nb):
        z = _dot(f_ref[:, b * w:(b + 1) * w].astype(BF16), d_ref[...])
        zz = jnp.concatenate([z[:, :w], z[:, w:]], axis=0).astype(BF16)
        o_ref[b] = _dot(t_ref[b], zz)


def _fft2_kernel(re_ref, im_ref, t_ref, o_ref):
    nb, _, w = re_ref.shape
    st = jnp.concatenate([re_ref[...].reshape(nb * 8, w), im_ref[...].reshape(nb * 8, w)], axis=0)
    o_ref[...] = _dot(t_ref[...], st.astype(BF16)).reshape(nb, 8, w)


def _fft_small_kernel(f_ref, t_ref, d_ref, o_ref):
    w = FNET_WIDTH
    z = _dot(f_ref[...].astype(BF16), d_ref[...])
    zz = jnp.concatenate([z[:, :w], z[:, w:]], axis=0).astype(BF16)
    o_ref[...] = _dot(t_ref[...], zz)


def _fourier_call(f, consts):
    n, w = f.shape
    if "t1" not in consts:
        return pl.pallas_call(
            _fft_small_kernel,
            out_shape=jax.ShapeDtypeStruct((n, w), F32),
            compiler_params=pltpu.CompilerParams(vmem_limit_bytes=V7X_VMEM_LIMIT_BYTES),
            name="fourier_small",
        )(f, consts["t"], consts["d"])
    a, b = _split_len(n)
    nb = 8
    mid = pl.pallas_call(
        functools.partial(_fft1_kernel, nb=nb),
        grid=(b // nb,),
        in_specs=[
            pl.BlockSpec((a, nb * w), lambda i: (0, i)),
            pl.BlockSpec((nb, 2 * a, 2 * a), lambda i: (i, 0, 0)),
            pl.BlockSpec((w, 2 * w), lambda i: (0, 0)),
        ],
        out_specs=pl.BlockSpec((nb, 2 * a, w), lambda i: (i, 0, 0)),
        out_shape=jax.ShapeDtypeStruct((b, 2 * a, w), F32),
        compiler_params=_params(("parallel",)),
        name="fourier_stage1",
    )(f.reshape(a, b * w), consts["t1"], consts["d"])
    mid = mid.reshape(b, 2, a, w)
    out = pl.pallas_call(
        _fft2_kernel,
        grid=(a // 8,),
        in_specs=[
            pl.BlockSpec((b, None, 8, w), lambda i: (0, 0, i, 0)),
            pl.BlockSpec((b, None, 8, w), lambda i: (0, 1, i, 0)),
            pl.BlockSpec((8 * b, 16 * b), lambda i: (0, 0)),
        ],
        out_specs=pl.BlockSpec((b, 8, w), lambda i: (0, i, 0)),
        out_shape=jax.ShapeDtypeStruct((b, a, w), F32),
        compiler_params=_params(("parallel",)),
        name="fourier_stage2",
    )(mid, mid, consts["t2"])
    return out.reshape(n, w)


def _fourier_consts(n):
    scale = 1.0 / math.sqrt(n * FNET_GROUP_DIM)
    d = jnp.asarray(_channel_dft(scale), BF16)
    if n <= 512:
        return {"d": d, "t": jnp.asarray(_dft_tables(n), BF16)}
    return {"d": d, "t1": jnp.asarray(_stage1_tables(n), BF16), "t2": jnp.asarray(_stage2_table(n), BF16)}


def _fill_window(win_ref, prev_ref, cur_ref, next_ref, tm):
    i = pl.program_id(0)
    last = pl.num_programs(0) - 1
    win_ref[0:HALO, :] = jnp.where(i > 0, prev_ref[tm - HALO:tm, :], 0.0)
    win_ref[HALO:HALO + tm, :] = cur_ref[...]
    win_ref[HALO + tm:HALO + tm + HALO, :] = jnp.where(i < last, next_ref[0:HALO, :], 0.0)


def _depthwise(win, w_ref, taps, rows):
    first = HALO - (taps - 1) // 2
    n = win.shape[0]
    acc = None
    for r in range(8):
        offs = [o for o in range(first, first + taps) if o % 8 == r]
        if not offs:
            continue
        base = win if r == 0 else pltpu.roll(win, n - r, 0)
        for o in offs:
            term = base[o - r:o - r + rows] * w_ref[o - first:o - first + 1, :]
            acc = term if acc is None else acc + term
    return acc


def _merge_kernel(x_ref, mod_ref, g_ref, attn_ref, fn_ref, hp_ref, hc_ref, hn_ref, sp_ref, sc_ref, sn_ref,
                  gb_ref, wg_ref, bg_ref, wa_ref, wf_ref, wc_ref, ws_ref, wo_ref,
                  dww_ref, dwb_ref, lng_ref, lnb_ref, scw_ref, o_ref, hwin_ref, swin_ref):
    tm, d = x_ref.shape
    _fill_window(hwin_ref, hp_ref, hc_ref, hn_ref, tm)
    _fill_window(swin_ref, sp_ref, sc_ref, sn_ref, tm)
    sub = tm // MERGE_SPLIT
    for part in range(MERGE_SPLIT):
        r0 = part * sub
        rows = slice(r0, r0 + sub)
        x = x_ref[rows, :]
        u = _modnorm(x, g_ref[...], mod_ref[0:1, :], mod_ref[1:2, :]).astype(BF16)

        c = _depthwise(hwin_ref[r0:r0 + sub + 2 * HALO, :], dww_ref, CONV_KERNEL, sub) + dwb_ref[...]
        mu = jnp.mean(c, axis=-1, keepdims=True)
        cc = c - mu
        var = jnp.mean(cc * cc, axis=-1, keepdims=True)
        conf = _silu((cc * lax.rsqrt(var + EPS)) * lng_ref[...] + lnb_ref[...])
        short = gb_ref[rows, :] * _depthwise(swin_ref[r0:r0 + sub + 2 * HALO, :], scw_ref, SC_KERNEL, sub)

        branches = (
            (attn_ref[rows, :], wa_ref),
            (fn_ref[rows, :].astype(BF16), wf_ref),
            (conf.astype(BF16), wc_ref),
            (short.astype(BF16), ws_ref),
        )
        merged = None
        for i, (val, w_ref) in enumerate(branches):
            cols = slice(i * d, (i + 1) * d)
            gate = jax.nn.sigmoid(_dot(u, wg_ref[:, cols]) + bg_ref[:, cols])
            term = gate * _dot(val, w_ref[...])
            merged = term if merged is None else merged + term
        o_ref[rows, :] = x + mod_ref[2:3, :] * _dot(merged.astype(BF16), wo_ref[...])


def _merge_call(x, mod3, g, attn, fn, hglu, sprod, gb, lw, *, tm_target):
    n, d = x.shape
    tm = _tile(n, tm_target)
    nt = n // tm
    row = lambda i: (i, 0)
    prev = lambda i: (jnp.maximum(i - 1, 0), 0)
    nxt = lambda i: (jnp.minimum(i + 1, nt - 1), 0)
    fix = lambda i: (0, 0)
    w = CONV_WIDTH

    def whole(arr):
        return pl.BlockSpec(arr.shape, fix, pipeline_mode=pl.Buffered(1))

    weights = (lw["wg"], lw["bg"], lw["wa"], lw["wf"], lw["wc"], lw["ws"], lw["wo"],
               lw["dww"], lw["dwb"], lw["lng"], lw["lnb"], lw["scw"])
    return pl.pallas_call(
        _merge_kernel,
        grid=(nt,),
        in_specs=[
            pl.BlockSpec((tm, d), row),
            pl.BlockSpec((3, d), fix),
            pl.BlockSpec((1, d), fix),
            pl.BlockSpec((tm, Q_W), row),
            pl.BlockSpec((tm, FNET_WIDTH), row),
            pl.BlockSpec((tm, w), prev), pl.BlockSpec((tm, w), row), pl.BlockSpec((tm, w), nxt),
            pl.BlockSpec((tm, w), prev), pl.BlockSpec((tm, w), row), pl.BlockSpec((tm, w), nxt),
            pl.BlockSpec((tm, w), row),
        ] + [whole(a) for a in weights],
        out_specs=pl.BlockSpec((tm, d), row),
        out_shape=jax.ShapeDtypeStruct((n, d), F32),
        scratch_shapes=[pltpu.VMEM((tm + 2 * HALO, w), F32), pltpu.VMEM((tm + 2 * HALO, w), F32)],
        compiler_params=_params(("arbitrary",)),
        name="mix_merge",
    )(x, mod3, g, attn, fn, hglu, hglu, hglu, sprod, sprod, sprod, gb, *weights)


def _rope_tables_t(n_tokens):
    n_rows = n_tokens // GRID_W
    row = jnp.broadcast_to(jnp.arange(n_rows)[:, None], (n_rows, GRID_W)).reshape(-1)
    col = jnp.broadcast_to(jnp.arange(GRID_W)[None, :], (n_rows, GRID_W)).reshape(-1)
    axis_dim = HEAD_DIM // 2
    inv_freq = ROPE_THETA ** (-jnp.arange(0, axis_dim, 2, dtype=F32) / axis_dim)
    pos = jnp.stack([row, col], axis=-1).astype(F32)
    ang = pos[:, :, None] * inv_freq
    ang_t = ang.reshape(n_tokens, axis_dim).T
    return jnp.cos(ang_t), jnp.sin(ang_t)


def kernel(x, c, ctx, c_ctx, w_ada, b_ada, norm_g, ffn_w13, ffn_w2, w_in, b_gate, q_norm_g, k_norm_g,
           w_attn_out, w_fnet, conv_dw_w, conv_dw_b, conv_ln_g, conv_ln_b, w_conv_out, sc_conv_w,
           w_sc_out, w_o, final_norm_g):
    batch, seq, d = x.shape
    assert batch == 1 and d == D_MODEL
    n_ctx = ctx.shape[1]
    depth = w_ada.shape[0]

    xs = x[0]
    cs = ctx[0]
    cos_x, sin_x = _rope_tables_t(seq)
    cos_c = jnp.ones((HEAD_DIM // 2, n_ctx), F32)
    sin_c = jnp.zeros((HEAD_DIM // 2, n_ctx), F32)
    fc_x = _fourier_consts(seq)
    fc_c = _fourier_consts(n_ctx)

    cc = jnp.zeros((8, d), F32).at[0].set(c[0]).at[1].set(c_ctx)
    mod = _mod_call(cc, w_ada, b_ada)
    mod = mod.reshape(depth, 8, N_MOD, d)

    nf = D_FF // FF_CHUNK
    w13 = ffn_w13.astype(BF16).reshape(depth, 2, d, 2 * nf, FF_CHUNK).transpose(0, 1, 3, 2, 4)
    w2 = ffn_w2.astype(BF16).reshape(depth, 2, nf, FF_CHUNK, d)
    wqkv_t = jnp.swapaxes(w_in[:, :, :QKV_W], 1, 2).astype(BF16)
    wrest = w_in[:, :, QKV_W:QKV_W + REST_W].astype(BF16)
    wg = w_in[:, :, QKV_W + REST_W:].astype(BF16)
    fg = final_norm_g.reshape(1, d)

    for l in range(depth):
        last = l == depth - 1
        mod_x, mod_c = mod[l, 0], mod[l, 1]
        lw = {
            "wg": wg[l], "bg": b_gate[l].reshape(1, -1),
            "wa": w_attn_out[l].astype(BF16), "wf": w_fnet[l].astype(BF16),
            "wc": w_conv_out[l].astype(BF16), "ws": w_sc_out[l].astype(BF16), "wo": w_o[l].astype(BF16),
            "dww": conv_dw_w[l], "dwb": conv_dw_b[l].reshape(1, -1),
            "lng": conv_ln_g[l].reshape(1, -1), "lnb": conv_ln_b[l].reshape(1, -1), "scw": sc_conv_w[l],
        }
        g0, g1, g2 = (norm_g[l, s].reshape(1, d) for s in range(3))
        qg = q_norm_g[l].reshape(HEAD_DIM, 1)
        kg = k_norm_g[l].reshape(HEAD_DIM, 1)

        xs = _ffn_call(xs, mod_x[0:3], g0, w13[l, 0], w2[l, 0], fg, final=False, tm_target=1024)
        cs = _ffn_call(cs, mod_c[0:3], g0, w13[l, 0], w2[l, 0], fg, final=False, tm_target=1024)

        q_x, k_x, v_x, f_x, h_x, s_x, gb_x = _inproj_call(
            xs, mod_x[3:5], g1, wqkv_t[l], wrest[l], qg, kg, cos_x, sin_x, tm_target=512)
        q_c, k_c, v_c, f_c, h_c, s_c, gb_c = _inproj_call(
            cs, mod_c[3:5], g1, wqkv_t[l], wrest[l], qg, kg, cos_c, sin_c, tm_target=512)

        k_all = jnp.concatenate([k_x, k_c], axis=0)
        v_all = jnp.concatenate([v_x, v_c], axis=1)
        bound = 8.0 * jnp.max(jnp.abs(q_norm_g[l])) * jnp.max(jnp.abs(k_norm_g[l]))
        safe = (bound <= SAFE_SCORE_BOUND).astype(jnp.int32).reshape(1)
        attn_x = _attn_call(safe, q_x, k_all, v_all, tq_target=1024, tk_target=1280)
        fn_x = _fourier_call(f_x, fc_x)
        xs = _merge_call(xs, mod_x[3:6], g1, attn_x, fn_x, h_x, s_x, gb_x, lw, tm_target=512)

        if not last:
            attn_c = _attn_call(safe, q_c, k_c, v_c, tq_target=1024, tk_target=1280)
            fn_c = _fourier_call(f_c, fc_c)
            cs = _merge_call(cs, mod_c[3:6], g1, attn_c, fn_c, h_c, s_c, gb_c, lw, tm_target=512)

        xs = _ffn_call(xs, mod_x[6:9], g2, w13[l, 1], w2[l, 1], fg, final=last, tm_target=1024)
        if not last:
            cs = _ffn_call(cs, mod_c[6:9], g2, w13[l, 1], w2[l, 1], fg, final=False, tm_target=1024)

    return xs[None]
```

```python
import functools
import math

import numpy as np
import jax
import jax.numpy as jnp
from jax import lax
from jax.experimental import pallas as pl
from jax.experimental.pallas import tpu as pltpu

F32 = jnp.float32
BF16 = jnp.bfloat16

D_MODEL = 1024
GRID_W = 64
N_HEADS = 8
N_KV_HEADS = 2
HEAD_DIM = 64
Q_GROUP = N_HEADS // N_KV_HEADS
Q_W = N_HEADS * HEAD_DIM
KV_W = N_KV_HEADS * HEAD_DIM
ROPE_THETA = 10000.0
FNET_GROUPS = 4
FNET_GROUP_DIM = 64
FNET_WIDTH = FNET_GROUPS * FNET_GROUP_DIM
CONV_WIDTH = 256
CONV_KERNEL = 31
SC_WIDTH = 256
SC_KERNEL = 3
N_BRANCHES = 4
D_FF = 2816
N_MOD = 9
EPS = 1e-6
QKV_W = Q_W + 2 * KV_W
REST_W = FNET_WIDTH + 2 * CONV_WIDTH + 3 * SC_WIDTH
V_ROWS = HEAD_DIM + 16
Q_SCALE = HEAD_DIM ** -0.5 * math.log2(math.e)
SAFE_SCORE_BOUND = 40.0
HALO = 16
FF_CHUNK = 256
MERGE_SPLIT = 2
V7X_VMEM_LIMIT_BYTES = 56 * 1024 * 1024


def _tile(n, target):
    if n <= target:
        return n
    t = (target // 128) * 128
    while t >= 128:
        if n % t == 0:
            return t
        t -= 128
    return n


def _params(sem):
    return pltpu.CompilerParams(dimension_semantics=sem, vmem_limit_bytes=V7X_VMEM_LIMIT_BYTES)


def _dot(a, b):
    return jnp.dot(a, b, preferred_element_type=F32)


def _dot_nt(a, b):
    return lax.dot_general(a, b, (((1,), (1,)), ((), ())), preferred_element_type=F32)


def _modnorm(x, g, shift, scale):
    y = x * lax.rsqrt(jnp.mean(x * x, axis=-1, keepdims=True) + EPS)
    return (y * g) * (1.0 + scale) + shift


def _silu(x):
    return x * jax.nn.sigmoid(x)


def _mod_kernel(c_ref, w_ref, b_ref, o_ref):
    a = _silu(c_ref[...])
    o_ref[0] = _dot(a.astype(BF16), w_ref[0].astype(BF16)) + b_ref[0]


def _mod_call(cc, w_ada, b_ada):
    depth, d, n = w_ada.shape
    tn = 1152
    return pl.pallas_call(
        _mod_kernel,
        grid=(depth, n // tn),
        in_specs=[
            pl.BlockSpec((8, d), lambda l, j: (0, 0)),
            pl.BlockSpec((1, d, tn), lambda l, j: (l, 0, j)),
            pl.BlockSpec((1, 1, tn), lambda l, j: (l, 0, j)),
        ],
        out_specs=pl.BlockSpec((1, 8, tn), lambda l, j: (l, 0, j)),
        out_shape=jax.ShapeDtypeStruct((depth, 8, n), F32),
        compiler_params=_params(("parallel", "parallel")),
        name="adaln_mod",
    )(cc, w_ada, b_ada.reshape(depth, 1, n))


def _ffn_kernel(x_ref, mod_ref, g_ref, w13_ref, w2_ref, fg_ref, o_ref, *, final):
    x = x_ref[...]
    h = _modnorm(x, g_ref[...], mod_ref[0:1, :], mod_ref[1:2, :]).astype(BF16)
    acc = None
    for c in range(D_FF // FF_CHUNK):
        lo, hi = c * FF_CHUNK, (c + 1) * FF_CHUNK
        a = _dot(h, w13_ref[:, lo:hi])
        b = _dot(h, w13_ref[:, D_FF + lo:D_FF + hi])
        t = _dot((_silu(a) * b).astype(BF16), w2_ref[lo:hi, :])
        acc = t if acc is None else acc + t
    y = x + (0.5 * mod_ref[2:3, :]) * acc
    if final:
        y = (y * lax.rsqrt(jnp.mean(y * y, axis=-1, keepdims=True) + EPS)) * fg_ref[...]
    o_ref[...] = y


def _ffn_call(x, mod3, g, w13, w2, fg, *, final, tm_target):
    n, d = x.shape
    tm = _tile(n, tm_target)
    fix = lambda i: (0, 0)
    return pl.pallas_call(
        functools.partial(_ffn_kernel, final=final),
        grid=(n // tm,),
        in_specs=[
            pl.BlockSpec((tm, d), lambda i: (i, 0)),
            pl.BlockSpec((3, d), fix),
            pl.BlockSpec((1, d), fix),
            pl.BlockSpec(w13.shape, fix, pipeline_mode=pl.Buffered(1)),
            pl.BlockSpec(w2.shape, fix, pipeline_mode=pl.Buffered(1)),
            pl.BlockSpec((1, d), fix),
        ],
        out_specs=pl.BlockSpec((tm, d), lambda i: (i, 0)),
        out_shape=jax.ShapeDtypeStruct((n, d), F32),
        compiler_params=_params(("parallel",)),
        name="ffn_half",
    )(x, mod3, g, w13, w2, fg)


def _head_norm_rope(t, gain, cos, sin, scale):
    ms = jnp.sum(t * t, axis=0, keepdims=True) * (1.0 / HEAD_DIM)
    t = (t * lax.rsqrt(ms + EPS)) * gain
    quarter = HEAD_DIM // 4
    parts = []
    for axis in range(2):
        x1 = t[2 * axis * quarter:(2 * axis + 1) * quarter]
        x2 = t[(2 * axis + 1) * quarter:(2 * axis + 2) * quarter]
        c = cos[axis * quarter:(axis + 1) * quarter]
        s = sin[axis * quarter:(axis + 1) * quarter]
        parts += [x1 * c - x2 * s, x1 * s + x2 * c]
    out = jnp.concatenate(parts, axis=0)
    return out * scale if scale != 1.0 else out


def _inproj_kernel(x_ref, mod_ref, g_ref, wqkv_ref, wrest_ref, qg_ref, kg_ref, cos_ref, sin_ref,
                   q_ref, k_ref, v_ref, f_ref, h_ref, s_ref, gb_ref):
    u = _modnorm(x_ref[...], g_ref[...], mod_ref[0:1, :], mod_ref[1:2, :]).astype(BF16)
    qkv = _dot_nt(wqkv_ref[...], u)
    cos = cos_ref[...]
    sin = sin_ref[...]
    zeros = jnp.zeros((HEAD_DIM, qkv.shape[1]), F32)
    for h in range(N_HEADS):
        t = _head_norm_rope(qkv[h * HEAD_DIM:(h + 1) * HEAD_DIM], qg_ref[...], cos, sin, Q_SCALE)
        blk = [t, zeros] if h // Q_GROUP == 0 else [zeros, t]
        q_ref[h * KV_W:(h + 1) * KV_W, :] = jnp.concatenate(blk, axis=0).astype(BF16)
    kt = [_head_norm_rope(qkv[Q_W + h * HEAD_DIM:Q_W + (h + 1) * HEAD_DIM], kg_ref[...], cos, sin, 1.0)
          for h in range(N_KV_HEADS)]
    k_ref[...] = jnp.concatenate(kt, axis=0).T.astype(BF16)
    ones = jnp.ones((V_ROWS - HEAD_DIM, qkv.shape[1]), F32)
    vt = []
    for h in range(N_KV_HEADS):
        vt += [qkv[Q_W + KV_W + h * HEAD_DIM:Q_W + KV_W + (h + 1) * HEAD_DIM], ones]
    v_ref[...] = jnp.concatenate(vt, axis=0).astype(BF16)

    r = _dot(u, wrest_ref[...])
    w = FNET_WIDTH
    f_ref[...] = r[:, 0:w]
    h_ref[...] = r[:, w:2 * w] * jax.nn.sigmoid(r[:, 2 * w:3 * w])
    gb_ref[...] = r[:, 3 * w:4 * w]
    s_ref[...] = r[:, 4 * w:5 * w] * r[:, 5 * w:6 * w]


def _inproj_call(x, mod2, g, wqkv_t, wrest, qg, kg, cos_t, sin_t, *, tm_target):
    n, d = x.shape
    tm = _tile(n, tm_target)
    row = lambda i: (i, 0)
    col = lambda i: (0, i)
    fix = lambda i: (0, 0)
    w = FNET_WIDTH
    return pl.pallas_call(
        _inproj_kernel,
        grid=(n // tm,),
        in_specs=[
            pl.BlockSpec((tm, d), row),
            pl.BlockSpec((2, d), fix),
            pl.BlockSpec((1, d), fix),
            pl.BlockSpec((QKV_W, d), fix),
            pl.BlockSpec((d, REST_W), fix),
            pl.BlockSpec((HEAD_DIM, 1), fix),
            pl.BlockSpec((HEAD_DIM, 1), fix),
            pl.BlockSpec((HEAD_DIM // 2, tm), col),
            pl.BlockSpec((HEAD_DIM // 2, tm), col),
        ],
        out_specs=[
            pl.BlockSpec((N_HEADS * KV_W, tm), col),
            pl.BlockSpec((tm, KV_W), row),
            pl.BlockSpec((N_KV_HEADS * V_ROWS, tm), col),
            pl.BlockSpec((tm, w), row),
            pl.BlockSpec((tm, w), row),
            pl.BlockSpec((tm, w), row),
            pl.BlockSpec((tm, w), row),
        ],
        out_shape=[
            jax.ShapeDtypeStruct((N_HEADS * KV_W, n), BF16),
            jax.ShapeDtypeStruct((n, KV_W), BF16),
            jax.ShapeDtypeStruct((N_KV_HEADS * V_ROWS, n), BF16),
            jax.ShapeDtypeStruct((n, w), F32),
            jax.ShapeDtypeStruct((n, w), F32),
            jax.ShapeDtypeStruct((n, w), F32),
            jax.ShapeDtypeStruct((n, w), F32),
        ],
        compiler_params=_params(("parallel",)),
        name="in_proj",
    )(x, mod2, g, wqkv_t, wrest, qg, kg, cos_t, sin_t)


def _attn_kernel(safe_ref, q_ref, k_ref, v_ref, o_ref, m_ref, acc_ref):
    n_chunks = k_ref.shape[0]
    acc_ref[...] = jnp.zeros_like(acc_ref)

    def scores(j, h):
        return _dot(k_ref[j], q_ref[h * KV_W:(h + 1) * KV_W, :])

    @pl.when(safe_ref[0] != 0)
    def _():
        def chunk(j, carry):
            for h in range(Q_GROUP):
                rows = slice(h * V_ROWS, (h + 1) * V_ROWS)
                acc_ref[rows, :] += _dot(v_ref[j], jnp.exp2(scores(j, h)).astype(BF16))
            return carry

        lax.fori_loop(0, n_chunks, chunk, 0)

    @pl.when(safe_ref[0] == 0)
    def _():
        m_ref[...] = jnp.full_like(m_ref, -jnp.inf)

        def chunk(j, carry):
            for h in range(Q_GROUP):
                s = scores(j, h)
                m_old = m_ref[h:h + 1, :]
                m_new = jnp.maximum(m_old, jnp.max(s, axis=0, keepdims=True))
                rows = slice(h * V_ROWS, (h + 1) * V_ROWS)
                acc_ref[rows, :] = (jnp.exp2(m_old - m_new) * acc_ref[rows, :]
                                    + _dot(v_ref[j], jnp.exp2(s - m_new).astype(BF16)))
                m_ref[h:h + 1, :] = m_new
            return carry

        lax.fori_loop(0, n_chunks, chunk, 0)

    outs = []
    for h in range(Q_GROUP):
        r0 = h * V_ROWS
        outs.append(acc_ref[r0:r0 + HEAD_DIM, :] / acc_ref[r0 + HEAD_DIM:r0 + HEAD_DIM + 1, :])
    o_ref[...] = jnp.concatenate(outs, axis=0).T.astype(BF16)


def _attn_call(safe, q_t, k, v_t, *, tq_target, tk_target):
    nq = q_t.shape[1]
    nk = k.shape[0]
    tq = _tile(nq, tq_target)
    tk = _tile(nk, tk_target)
    n_chunks = nk // tk
    gw = Q_GROUP * HEAD_DIM
    k3 = k.reshape(n_chunks, tk, KV_W)
    v4 = v_t.reshape(N_KV_HEADS, V_ROWS, n_chunks, tk).transpose(0, 2, 1, 3)
    return pl.pallas_call(
        _attn_kernel,
        grid=(N_KV_HEADS, nq // tq),
        in_specs=[
            pl.BlockSpec(memory_space=pltpu.SMEM),
            pl.BlockSpec((Q_GROUP * KV_W, tq), lambda g, i: (g, i)),
            pl.BlockSpec((n_chunks, tk, KV_W), lambda g, i: (0, 0, 0)),
            pl.BlockSpec((None, n_chunks, V_ROWS, tk), lambda g, i: (g, 0, 0, 0)),
        ],
        out_specs=pl.BlockSpec((tq, gw), lambda g, i: (i, g)),
        out_shape=jax.ShapeDtypeStruct((nq, Q_W), BF16),
        scratch_shapes=[pltpu.VMEM((8, tq), F32), pltpu.VMEM((Q_GROUP * V_ROWS, tq), F32)],
        compiler_params=_params(("parallel", "parallel")),
        name="attention",
    )(safe, q_t, k3, v4)


def _split_len(n):
    b = 1 << (int(math.log2(n)) // 2)
    return n // b, b


def _channel_dft(scale):
    c = np.arange(FNET_GROUP_DIM)
    ang = 2.0 * np.pi * np.outer(c, c) / FNET_GROUP_DIM
    eye = np.eye(FNET_GROUPS)
    return np.concatenate([np.kron(eye, np.cos(ang)), -np.kron(eye, np.sin(ang))], axis=1) * scale


def _stage1_tables(n):
    a, b = _split_len(n)
    c = np.arange(a)[None, :, None]
    pos = b * np.arange(a)[None, None, :] + np.arange(b)[:, None, None]
    ang = 2.0 * np.pi * ((c * pos) % n) / n
    co, si = np.cos(ang), np.sin(ang)
    return np.concatenate([np.concatenate([co, si], axis=2), np.concatenate([-si, co], axis=2)], axis=1)


def _stage2_table(n):
    a, b = _split_len(n)
    ang = 2.0 * np.pi * np.outer(np.arange(b), np.arange(b)) / b
    eye = np.eye(8)
    return np.concatenate([np.kron(np.cos(ang), eye), np.kron(np.sin(ang), eye)], axis=1)


def _dft_tables(n):
    ang = 2.0 * np.pi * np.outer(np.arange(n), np.arange(n)) / n
    return np.concatenate([np.cos(ang), np.sin(ang)], axis=1)


def _fft1_kernel(f_ref, t_ref, d_ref, o_ref, *, nb):
    w = FNET_WIDTH
    for b in range(nb):
        z = _dot(f_ref[:, b * w:(b + 1) * w].astype(BF16), d_ref[...])
        zz = jnp.concatenate([z[:, :w], z[:, w:]], axis=0).astype(BF16)
        o_ref[b] = _dot(t_ref[b], zz)


def _fft2_kernel(re_ref, im_ref, t_ref, o_ref):
    nb, _, w = re_ref.shape
    st = jnp.concatenate([re_ref[...].reshape(nb * 8, w), im_ref[...].reshape(nb * 8, w)], axis=0)
    o_ref[...] = _dot(t_ref[...], st.astype(BF16)).reshape(nb, 8, w)


def _fft_small_kernel(f_ref, t_ref, d_ref, o_ref):
    w = FNET_WIDTH
    z = _dot(f_ref[...].astype(BF16), d_ref[...])
    zz = jnp.concatenate([z[:, :w], z[:, w:]], axis=0).astype(BF16)
    o_ref[...] = _dot(t_ref[...], zz)


def _fourier_call(f, consts):
    n, w = f.shape
    if "t1" not in consts:
        return pl.pallas_call(
            _fft_small_kernel,
            out_shape=jax.ShapeDtypeStruct((n, w), F32),
            compiler_params=pltpu.CompilerParams(vmem_limit_bytes=V7X_VMEM_LIMIT_BYTES),
            name="fourier_small",
        )(f, consts["t"], consts["d"])
    a, b = _split_len(n)
    nb = 8
    mid = pl.pallas_call(
        functools.partial(_fft1_kernel, nb=nb),
        grid=(b // nb,),
        in_specs=[
            pl.BlockSpec((a, nb * w), lambda i: (0, i)),
            pl.BlockSpec((nb, 2 * a, 2 * a), lambda i: (i, 0, 0)),
            pl.BlockSpec((w, 2 * w), lambda i: (0, 0)),
        ],
        out_specs=pl.BlockSpec((nb, 2 * a, w), lambda i: (i, 0, 0)),
        out_shape=jax.ShapeDtypeStruct((b, 2 * a, w), F32),
        compiler_params=_params(("parallel",)),
        name="fourier_stage1",
    )(f.reshape(a, b * w), consts["t1"], consts["d"])
    mid = mid.reshape(b, 2, a, w)
    out = pl.pallas_call(
        _fft2_kernel,
        grid=(a // 8,),
        in_specs=[
            pl.BlockSpec((b, None, 8, w), lambda i: (0, 0, i, 0)),
            pl.BlockSpec((b, None, 8, w), lambda i: (0, 1, i, 0)),
            pl.BlockSpec((8 * b, 16 * b), lambda i: (0, 0)),
        ],
        out_specs=pl.BlockSpec((b, 8, w), lambda i: (0, i, 0)),
        out_shape=jax.ShapeDtypeStruct((b, a, w), F32),
        compiler_params=_params(("parallel",)),
        name="fourier_stage2",
    )(mid, mid, consts["t2"])
    return out.reshape(n, w)


def _fourier_consts(n):
    scale = 1.0 / math.sqrt(n * FNET_GROUP_DIM)
    d = jnp.asarray(_channel_dft(scale), BF16)
    if n <= 512:
        return {"d": d, "t": jnp.asarray(_dft_tables(n), BF16)}
    return {"d": d, "t1": jnp.asarray(_stage1_tables(n), BF16), "t2": jnp.asarray(_stage2_table(n), BF16)}


def _fill_window(win_ref, prev_ref, cur_ref, next_ref, tm):
    i = pl.program_id(0)
    last = pl.num_programs(0) - 1
    win_ref[0:HALO, :] = jnp.where(i > 0, prev_ref[tm - HALO:tm, :], 0.0)
    win_ref[HALO:HALO + tm, :] = cur_ref[...]
    win_ref[HALO + tm:HALO + tm + HALO, :] = jnp.where(i < last, next_ref[0:HALO, :], 0.0)


def _depthwise(win, w_ref, taps, rows):
    first = HALO - (taps - 1) // 2
    n = win.shape[0]
    acc = None
    for r in range(8):
        offs = [o for o in range(first, first + taps) if o % 8 == r]
        if not offs:
            continue
        base = win if r == 0 else pltpu.roll(win, n - r, 0)
        for o in offs:
            term = base[o - r:o - r + rows] * w_ref[o - first:o - first + 1, :]
            acc = term if acc is None else acc + term
    return acc


def _merge_kernel(x_ref, mod_ref, g_ref, attn_ref, fn_ref, hp_ref, hc_ref, hn_ref, sp_ref, sc_ref, sn_ref,
                  gb_ref, wg_ref, bg_ref, wa_ref, wf_ref, wc_ref, ws_ref, wo_ref,
                  dww_ref, dwb_ref, lng_ref, lnb_ref, scw_ref, o_ref, hwin_ref, swin_ref):
    tm, d = x_ref.shape
    _fill_window(hwin_ref, hp_ref, hc_ref, hn_ref, tm)
    _fill_window(swin_ref, sp_ref, sc_ref, sn_ref, tm)
    sub = tm // MERGE_SPLIT
    for part in range(MERGE_SPLIT):
        r0 = part * sub
        rows = slice(r0, r0 + sub)
        x = x_ref[rows, :]
        u = _modnorm(x, g_ref[...], mod_ref[0:1, :], mod_ref[1:2, :]).astype(BF16)

        c = _depthwise(hwin_ref[r0:r0 + sub + 2 * HALO, :], dww_ref, CONV_KERNEL, sub) + dwb_ref[...]
        mu = jnp.mean(c, axis=-1, keepdims=True)
        cc = c - mu
        var = jnp.mean(cc * cc, axis=-1, keepdims=True)
        conf = _silu((cc * lax.rsqrt(var + EPS)) * lng_ref[...] + lnb_ref[...])
        short = gb_ref[rows, :] * _depthwise(swin_ref[r0:r0 + sub + 2 * HALO, :], scw_ref, SC_KERNEL, sub)

        branches = (
            (attn_ref[rows, :], wa_ref),
            (fn_ref[rows, :].astype(BF16), wf_ref),
            (conf.astype(BF16), wc_ref),
            (short.astype(BF16), ws_ref),
        )
        merged = None
        for i, (val, w_ref) in enumerate(branches):
            cols = slice(i * d, (i + 1) * d)
            gate = jax.nn.sigmoid(_dot(u, wg_ref[:, cols]) + bg_ref[:, cols])
            term = gate * _dot(val, w_ref[...])
            merged = term if merged is None else merged + term
        o_ref[rows, :] = x + mod_ref[2:3, :] * _dot(merged.astype(BF16), wo_ref[...])


def _merge_call(x, mod3, g, attn, fn, hglu, sprod, gb, lw, *, tm_target):
    n, d = x.shape
    tm = _tile(n, tm_target)
    nt = n // tm
    row = lambda i: (i, 0)
    prev = lambda i: (jnp.maximum(i - 1, 0), 0)
    nxt = lambda i: (jnp.minimum(i + 1, nt - 1), 0)
    fix = lambda i: (0, 0)
    w = CONV_WIDTH

    def whole(arr):
        return pl.BlockSpec(arr.shape, fix, pipeline_mode=pl.Buffered(1))

    weights = (lw["wg"], lw["bg"], lw["wa"], lw["wf"], lw["wc"], lw["ws"], lw["wo"],
               lw["dww"], lw["dwb"], lw["lng"], lw["lnb"], lw["scw"])
    return pl.pallas_call(
        _merge_kernel,
        grid=(nt,),
        in_specs=[
            pl.BlockSpec((tm, d), row),
            pl.BlockSpec((3, d), fix),
            pl.BlockSpec((1, d), fix),
            pl.BlockSpec((tm, Q_W), row),
            pl.BlockSpec((tm, FNET_WIDTH), row),
            pl.BlockSpec((tm, w), prev), pl.BlockSpec((tm, w), row), pl.BlockSpec((tm, w), nxt),
            pl.BlockSpec((tm, w), prev), pl.BlockSpec((tm, w), row), pl.BlockSpec((tm, w), nxt),
            pl.BlockSpec((tm, w), row),
        ] + [whole(a) for a in weights],
        out_specs=pl.BlockSpec((tm, d), row),
        out_shape=jax.ShapeDtypeStruct((n, d), F32),
        scratch_shapes=[pltpu.VMEM((tm + 2 * HALO, w), F32), pltpu.VMEM((tm + 2 * HALO, w), F32)],
        compiler_params=_params(("arbitrary",)),
        name="mix_merge",
    )(x, mod3, g, attn, fn, hglu, hglu, hglu, sprod, sprod, sprod, gb, *weights)


def _rope_tables_t(n_tokens):
    n_rows = n_tokens // GRID_W
    row = jnp.broadcast_to(jnp.arange(n_rows)[:, None], (n_rows, GRID_W)).reshape(-1)
    col = jnp.broadcast_to(jnp.arange(GRID_W)[None, :], (n_rows, GRID_W)).reshape(-1)
    axis_dim = HEAD_DIM // 2
    inv_freq = ROPE_THETA ** (-jnp.arange(0, axis_dim, 2, dtype=F32) / axis_dim)
    pos = jnp.stack([row, col], axis=-1).astype(F32)
    ang = pos[:, :, None] * inv_freq
    ang_t = ang.reshape(n_tokens, axis_dim).T
    return jnp.cos(ang_t), jnp.sin(ang_t)


def kernel(x, c, ctx, c_ctx, w_ada, b_ada, norm_g, ffn_w13, ffn_w2, w_in, b_gate, q_norm_g, k_norm_g,
           w_attn_out, w_fnet, conv_dw_w, conv_dw_b, conv_ln_g, conv_ln_b, w_conv_out, sc_conv_w,
           w_sc_out, w_o, final_norm_g):
    batch, seq, d = x.shape
    assert batch == 1 and d == D_MODEL
    n_ctx = ctx.shape[1]
    depth = w_ada.shape[0]

    xs = x[0]
    cs = ctx[0]
    cos_x, sin_x = _rope_tables_t(seq)
    cos_c = jnp.ones((HEAD_DIM // 2, n_ctx), F32)
    sin_c = jnp.zeros((HEAD_DIM // 2, n_ctx), F32)
    fc_x = _fourier_consts(seq)
    fc_c = _fourier_consts(n_ctx)

    cc = jnp.zeros((8, d), F32).at[0].set(c[0]).at[1].set(c_ctx)
    mod = _mod_call(cc, w_ada, b_ada)
    mod = mod.reshape(depth, 8, N_MOD, d)

    w13 = ffn_w13.astype(BF16)
    w2 = ffn_w2.astype(BF16)
    wqkv_t = jnp.swapaxes(w_in[:, :, :QKV_W], 1, 2).astype(BF16)
    wrest = w_in[:, :, QKV_W:QKV_W + REST_W].astype(BF16)
    wg = w_in[:, :, QKV_W + REST_W:].astype(BF16)
    fg = final_norm_g.reshape(1, d)

    for l in range(depth):
        last = l == depth - 1
        mod_x, mod_c = mod[l, 0], mod[l, 1]
        lw = {
            "wg": wg[l], "bg": b_gate[l].reshape(1, -1),
            "wa": w_attn_out[l].astype(BF16), "wf": w_fnet[l].astype(BF16),
            "wc": w_conv_out[l].astype(BF16), "ws": w_sc_out[l].astype(BF16), "wo": w_o[l].astype(BF16),
            "dww": conv_dw_w[l], "dwb": conv_dw_b[l].reshape(1, -1),
            "lng": conv_ln_g[l].reshape(1, -1), "lnb": conv_ln_b[l].reshape(1, -1), "scw": sc_conv_w[l],
        }
        g0, g1, g2 = (norm_g[l, s].reshape(1, d) for s in range(3))
        qg = q_norm_g[l].reshape(HEAD_DIM, 1)
        kg = k_norm_g[l].reshape(HEAD_DIM, 1)

        xs = _ffn_call(xs, mod_x[0:3], g0, w13[l, 0], w2[l, 0], fg, final=False, tm_target=1024)
        cs = _ffn_call(cs, mod_c[0:3], g0, w13[l, 0], w2[l, 0], fg, final=False, tm_target=1024)

        q_x, k_x, v_x, f_x, h_x, s_x, gb_x = _inproj_call(
            xs, mod_x[3:5], g1, wqkv_t[l], wrest[l], qg, kg, cos_x, sin_x, tm_target=512)
        q_c, k_c, v_c, f_c, h_c, s_c, gb_c = _inproj_call(
            cs, mod_c[3:5], g1, wqkv_t[l], wrest[l], qg, kg, cos_c, sin_c, tm_target=512)

        k_all = jnp.concatenate([k_x, k_c], axis=0)
        v_all = jnp.concatenate([v_x, v_c], axis=1)
        bound = 8.0 * jnp.max(jnp.abs(q_norm_g[l])) * jnp.max(jnp.abs(k_norm_g[l]))
        safe = (bound <= SAFE_SCORE_BOUND).astype(jnp.int32).reshape(1)
        attn_x = _attn_call(safe, q_x, k_all, v_all, tq_target=1024, tk_target=1280)
        fn_x = _fourier_call(f_x, fc_x)
        xs = _merge_call(xs, mod_x[3:6], g1, attn_x, fn_x, h_x, s_x, gb_x, lw, tm_target=512)

        if not last:
            attn_c = _attn_call(safe, q_c, k_c, v_c, tq_target=1024, tk_target=1280)
            fn_c = _fourier_call(f_c, fc_c)
            cs = _merge_call(cs, mod_c[3:6], g1, attn_c, fn_c, h_c, s_c, gb_c, lw, tm_target=512)

        xs = _ffn_call(xs, mod_x[6:9], g2, w13[l, 1], w2[l, 1], fg, final=last, tm_target=1024)
        if not last:
            cs = _ffn_call(cs, mod_c[6:9], g2, w13[l, 1], w2[l, 1], fg, final=False, tm_target=1024)

    return xs[None]
```

```python
import functools
import math

import numpy as np
import jax
import jax.numpy as jnp
from jax import lax
from jax.experimental import pallas as pl
from jax.experimental.pallas import tpu as pltpu

F32 = jnp.float32
BF16 = jnp.bfloat16

D_MODEL = 1024
GRID_W = 64
N_HEADS = 8
N_KV_HEADS = 2
HEAD_DIM = 64
Q_GROUP = N_HEADS // N_KV_HEADS
Q_W = N_HEADS * HEAD_DIM
KV_W = N_KV_HEADS * HEAD_DIM
ROPE_THETA = 10000.0
FNET_GROUPS = 4
FNET_GROUP_DIM = 64
FNET_WIDTH = FNET_GROUPS * FNET_GROUP_DIM
CONV_WIDTH = 256
CONV_KERNEL = 31
SC_WIDTH = 256
SC_KERNEL = 3
N_BRANCHES = 4
D_FF = 2816
N_MOD = 9
EPS = 1e-6
QKV_W = Q_W + 2 * KV_W
REST_W = FNET_WIDTH + 2 * CONV_WIDTH + 3 * SC_WIDTH
V_ROWS = HEAD_DIM + 16
Q_SCALE = HEAD_DIM ** -0.5 * math.log2(math.e)
SAFE_SCORE_BOUND = 40.0
HALO = 16
FF_CHUNK = 256
MERGE_SPLIT = 2
V7X_VMEM_LIMIT_BYTES = 56 * 1024 * 1024


def _tile(n, target):
    if n <= target:
        return n
    t = (target // 128) * 128
    while t >= 128:
        if n % t == 0:
            return t
        t -= 128
    return n


def _params(sem):
    return pltpu.CompilerParams(dimension_semantics=sem, vmem_limit_bytes=V7X_VMEM_LIMIT_BYTES)


def _dot(a, b):
    return jnp.dot(a, b, preferred_element_type=F32)


def _dot_nt(a, b):
    return lax.dot_general(a, b, (((1,), (1,)), ((), ())), preferred_element_type=F32)


def _modnorm(x, g, shift, scale):
    y = x * lax.rsqrt(jnp.mean(x * x, axis=-1, keepdims=True) + EPS)
    return (y * g) * (1.0 + scale) + shift


def _silu(x):
    return x * jax.nn.sigmoid(x)


def _mod_kernel(c_ref, w_ref, b_ref, o_ref):
    a = _silu(c_ref[...])
    o_ref[0] = _dot(a.astype(BF16), w_ref[0].astype(BF16)) + b_ref[0]


def _mod_call(cc, w_ada, b_ada):
    depth, d, n = w_ada.shape
    tn = 1152
    return pl.pallas_call(
        _mod_kernel,
        grid=(depth, n // tn),
        in_specs=[
            pl.BlockSpec((8, d), lambda l, j: (0, 0)),
            pl.BlockSpec((1, d, tn), lambda l, j: (l, 0, j)),
            pl.BlockSpec((1, 1, tn), lambda l, j: (l, 0, j)),
        ],
        out_specs=pl.BlockSpec((1, 8, tn), lambda l, j: (l, 0, j)),
        out_shape=jax.ShapeDtypeStruct((depth, 8, n), F32),
        compiler_params=_params(("parallel", "parallel")),
        name="adaln_mod",
    )(cc, w_ada, b_ada.reshape(depth, 1, n))


def _ffn_kernel(x_ref, mod_ref, g_ref, w13_ref, w2_ref, fg_ref, o_ref, *, final):
    x = x_ref[...]
    h = _modnorm(x, g_ref[...], mod_ref[0:1, :], mod_ref[1:2, :]).astype(BF16)
    acc = None
    for c in range(D_FF // FF_CHUNK):
        lo, hi = c * FF_CHUNK, (c + 1) * FF_CHUNK
        a = _dot(h, w13_ref[:, lo:hi])
        b = _dot(h, w13_ref[:, D_FF + lo:D_FF + hi])
        t = _dot((_silu(a) * b).astype(BF16), w2_ref[lo:hi, :])
        acc = t if acc is None else acc + t
    y = x + (0.5 * mod_ref[2:3, :]) * acc
    if final:
        y = (y * lax.rsqrt(jnp.mean(y * y, axis=-1, keepdims=True) + EPS)) * fg_ref[...]
    o_ref[...] = y


def _ffn_call(x, mod3, g, w13, w2, fg, *, final, tm_target):
    n, d = x.shape
    tm = _tile(n, tm_target)
    fix = lambda i: (0, 0)
    return pl.pallas_call(
        functools.partial(_ffn_kernel, final=final),
        grid=(n // tm,),
        in_specs=[
            pl.BlockSpec((tm, d), lambda i: (i, 0)),
            pl.BlockSpec((3, d), fix),
            pl.BlockSpec((1, d), fix),
            pl.BlockSpec(w13.shape, fix, pipeline_mode=pl.Buffered(1)),
            pl.BlockSpec(w2.shape, fix, pipeline_mode=pl.Buffered(1)),
            pl.BlockSpec((1, d), fix),
        ],
        out_specs=pl.BlockSpec((tm, d), lambda i: (i, 0)),
        out_shape=jax.ShapeDtypeStruct((n, d), F32),
        compiler_params=_params(("parallel",)),
        name="ffn_half",
    )(x, mod3, g, w13, w2, fg)


def _head_norm_rope(t, gain, cos, sin, scale):
    ms = jnp.sum(t * t, axis=0, keepdims=True) * (1.0 / HEAD_DIM)
    t = (t * lax.rsqrt(ms + EPS)) * gain
    quarter = HEAD_DIM // 4
    parts = []
    for axis in range(2):
        x1 = t[2 * axis * quarter:(2 * axis + 1) * quarter]
        x2 = t[(2 * axis + 1) * quarter:(2 * axis + 2) * quarter]
        c = cos[axis * quarter:(axis + 1) * quarter]
        s = sin[axis * quarter:(axis + 1) * quarter]
        parts += [x1 * c - x2 * s, x1 * s + x2 * c]
    out = jnp.concatenate(parts, axis=0)
    return out * scale if scale != 1.0 else out


def _inproj_kernel(x_ref, mod_ref, g_ref, wqkv_ref, wrest_ref, qg_ref, kg_ref, cos_ref, sin_ref,
                   q_ref, k_ref, v_ref, f_ref, h_ref, s_ref, gb_ref):
    u = _modnorm(x_ref[...], g_ref[...], mod_ref[0:1, :], mod_ref[1:2, :]).astype(BF16)
    qkv = _dot_nt(wqkv_ref[...], u)
    cos = cos_ref[...]
    sin = sin_ref[...]
    zeros = jnp.zeros((HEAD_DIM, qkv.shape[1]), F32)
    for h in range(N_HEADS):
        t = _head_norm_rope(qkv[h * HEAD_DIM:(h + 1) * HEAD_DIM], qg_ref[...], cos, sin, Q_SCALE)
        blk = [t, zeros] if h // Q_GROUP == 0 else [zeros, t]
        q_ref[h * KV_W:(h + 1) * KV_W, :] = jnp.concatenate(blk, axis=0).astype(BF16)
    kt = [_head_norm_rope(qkv[Q_W + h * HEAD_DIM:Q_W + (h + 1) * HEAD_DIM], kg_ref[...], cos, sin, 1.0)
          for h in range(N_KV_HEADS)]
    k_ref[...] = jnp.concatenate(kt, axis=0).T.astype(BF16)
    ones = jnp.ones((V_ROWS - HEAD_DIM, qkv.shape[1]), F32)
    vt = []
    for h in range(N_KV_HEADS):
        vt += [qkv[Q_W + KV_W + h * HEAD_DIM:Q_W + KV_W + (h + 1) * HEAD_DIM], ones]
    v_ref[...] = jnp.concatenate(vt, axis=0).astype(BF16)

    r = _dot(u, wrest_ref[...])
    w = FNET_WIDTH
    f_ref[...] = r[:, 0:w]
    h_ref[...] = r[:, w:2 * w] * jax.nn.sigmoid(r[:, 2 * w:3 * w])
    gb_ref[...] = r[:, 3 * w:4 * w]
    s_ref[...] = r[:, 4 * w:5 * w] * r[:, 5 * w:6 * w]


def _inproj_call(x, mod2, g, wqkv_t, wrest, qg, kg, cos_t, sin_t, *, tm_target):
    n, d = x.shape
    tm = _tile(n, tm_target)
    row = lambda i: (i, 0)
    col = lambda i: (0, i)
    fix = lambda i: (0, 0)
    w = FNET_WIDTH
    return pl.pallas_call(
        _inproj_kernel,
        grid=(n // tm,),
        in_specs=[
            pl.BlockSpec((tm, d), row),
            pl.BlockSpec((2, d), fix),
            pl.BlockSpec((1, d), fix),
            pl.BlockSpec((QKV_W, d), fix),
            pl.BlockSpec((d, REST_W), fix),
            pl.BlockSpec((HEAD_DIM, 1), fix),
            pl.BlockSpec((HEAD_DIM, 1), fix),
            pl.BlockSpec((HEAD_DIM // 2, tm), col),
            pl.BlockSpec((HEAD_DIM // 2, tm), col),
        ],
        out_specs=[
            pl.BlockSpec((N_HEADS * KV_W, tm), col),
            pl.BlockSpec((tm, KV_W), row),
            pl.BlockSpec((N_KV_HEADS * V_ROWS, tm), col),
            pl.BlockSpec((tm, w), row),
            pl.BlockSpec((tm, w), row),
            pl.BlockSpec((tm, w), row),
            pl.BlockSpec((tm, w), row),
        ],
        out_shape=[
            jax.ShapeDtypeStruct((N_HEADS * KV_W, n), BF16),
            jax.ShapeDtypeStruct((n, KV_W), BF16),
            jax.ShapeDtypeStruct((N_KV_HEADS * V_ROWS, n), BF16),
            jax.ShapeDtypeStruct((n, w), F32),
            jax.ShapeDtypeStruct((n, w), F32),
            jax.ShapeDtypeStruct((n, w), F32),
            jax.ShapeDtypeStruct((n, w), F32),
        ],
        compiler_params=_params(("parallel",)),
        name="in_proj",
    )(x, mod2, g, wqkv_t, wrest, qg, kg, cos_t, sin_t)


def _attn_kernel(safe_ref, q_ref, k_ref, v_ref, o_ref, m_ref, acc_ref):
    n_chunks = k_ref.shape[0]
    acc_ref[...] = jnp.zeros_like(acc_ref)

    def scores(j, h):
        return _dot(k_ref[j], q_ref[h * KV_W:(h + 1) * KV_W, :])

    @pl.when(safe_ref[0] != 0)
    def _():
        def chunk(j, carry):
            for h in range(Q_GROUP):
                rows = slice(h * V_ROWS, (h + 1) * V_ROWS)
                acc_ref[rows, :] += _dot(v_ref[j], jnp.exp2(scores(j, h)).astype(BF16))
            return carry

        lax.fori_loop(0, n_chunks, chunk, 0)

    @pl.when(safe_ref[0] == 0)
    def _():
        m_ref[...] = jnp.full_like(m_ref, -jnp.inf)

        def chunk(j, carry):
            for h in range(Q_GROUP):
                s = scores(j, h)
                m_old = m_ref[h:h + 1, :]
                m_new = jnp.maximum(m_old, jnp.max(s, axis=0, keepdims=True))
                rows = slice(h * V_ROWS, (h + 1) * V_ROWS)
                acc_ref[rows, :] = (jnp.exp2(m_old - m_new) * acc_ref[rows, :]
                                    + _dot(v_ref[j], jnp.exp2(s - m_new).astype(BF16)))
                m_ref[h:h + 1, :] = m_new
            return carry

        lax.fori_loop(0, n_chunks, chunk, 0)

    outs = []
    for h in range(Q_GROUP):
        r0 = h * V_ROWS
        outs.append(acc_ref[r0:r0 + HEAD_DIM, :] / acc_ref[r0 + HEAD_DIM:r0 + HEAD_DIM + 1, :])
    o_ref[...] = jnp.concatenate(outs, axis=0).T.astype(BF16)


def _attn_call(safe, q_t, k, v_t, *, tq_target, tk_target):
    nq = q_t.shape[1]
    nk = k.shape[0]
    tq = _tile(nq, tq_target)
    tk = _tile(nk, tk_target)
    n_chunks = nk // tk
    gw = Q_GROUP * HEAD_DIM
    k3 = k.reshape(n_chunks, tk, KV_W)
    v4 = v_t.reshape(N_KV_HEADS, V_ROWS, n_chunks, tk).transpose(0, 2, 1, 3)
    return pl.pallas_call(
        _attn_kernel,
        grid=(N_KV_HEADS, nq // tq),
        in_specs=[
            pl.BlockSpec(memory_space=pltpu.SMEM),
            pl.BlockSpec((Q_GROUP * KV_W, tq), lambda g, i: (g, i)),
            pl.BlockSpec((n_chunks, tk, KV_W), lambda g, i: (0, 0, 0)),
            pl.BlockSpec((None, n_chunks, V_ROWS, tk), lambda g, i: (g, 0, 0, 0)),
        ],
        out_specs=pl.BlockSpec((tq, gw), lambda g, i: (i, g)),
        out_shape=jax.ShapeDtypeStruct((nq, Q_W), BF16),
        scratch_shapes=[pltpu.VMEM((8, tq), F32), pltpu.VMEM((Q_GROUP * V_ROWS, tq), F32)],
        compiler_params=_params(("parallel", "parallel")),
        name="attention",
    )(safe, q_t, k3, v4)


def _split_len(n):
    b = 1 << (int(math.log2(n)) // 2)
    return n // b, b


def _channel_dft(scale):
    c = np.arange(FNET_GROUP_DIM)
    ang = 2.0 * np.pi * np.outer(c, c) / FNET_GROUP_DIM
    eye = np.eye(FNET_GROUPS)
    return np.concatenate([np.kron(eye, np.cos(ang)), -np.kron(eye, np.sin(ang))], axis=1) * scale


def _stage1_tables(n):
    a, b = _split_len(n)
    c = np.arange(a)[None, :, None]
    pos = b * np.arange(a)[None, None, :] + np.arange(b)[:, None, None]
    ang = 2.0 * np.pi * ((c * pos) % n) / n
    co, si = np.cos(ang), np.sin(ang)
    return np.concatenate([np.concatenate([co, si], axis=2), np.concatenate([-si, co], axis=2)], axis=1)


def _stage2_table(n):
    a, b = _split_len(n)
    ang = 2.0 * np.pi * np.outer(np.arange(b), np.arange(b)) / b
    eye = np.eye(8)
    return np.concatenate([np.kron(np.cos(ang), eye), np.kron(np.sin(ang), eye)], axis=1)


def _dft_tables(n):
    ang = 2.0 * np.pi * np.outer(np.arange(n), np.arange(n)) / n
    return np.concatenate([np.cos(ang), np.sin(ang)], axis=1)


def _fft1_kernel(f_ref, t_ref, d_ref, o_ref, *, nb):
    w = FNET_WIDTH
    for b in range(nb):
        z = _dot(f_ref[:, b * w:(b + 1) * w].astype(BF16), d_ref[...])
        zz = jnp.concatenate([z[:, :w], z[:, w:]], axis=0).astype(BF16)
        o_ref[b] = _dot(t_ref[b], zz)


def _fft2_kernel(re_ref, im_ref, t_ref, o_ref):
    nb, _, w = re_ref.shape
    st = jnp.concatenate([re_ref[...].reshape(nb * 8, w), im_ref[...].reshape(nb * 8, w)], axis=0)
    o_ref[...] = _dot(t_ref[...], st.astype(BF16)).reshape(nb, 8, w)


def _fft_small_kernel(f_ref, t_ref, d_ref, o_ref):
    w = FNET_WIDTH
    z = _dot(f_ref[...].astype(BF16), d_ref[...])
    zz = jnp.concatenate([z[:, :w], z[:, w:]], axis=0).astype(BF16)
    o_ref[...] = _dot(t_ref[...], zz)


def _fourier_call(f, consts):
    n, w = f.shape
    if "t1" not in consts:
        return pl.pallas_call(
            _fft_small_kernel,
            out_shape=jax.ShapeDtypeStruct((n, w), F32),
            compiler_params=pltpu.CompilerParams(vmem_limit_bytes=V7X_VMEM_LIMIT_BYTES),
            name="fourier_small",
        )(f, consts["t"], consts["d"])
    a, b = _split_len(n)
    nb = 8
    mid = pl.pallas_call(
        functools.partial(_fft1_kernel, nb=nb),
        grid=(b // nb,),
        in_specs=[
            pl.BlockSpec((a, nb * w), lambda i: (0, i)),
            pl.BlockSpec((nb, 2 * a, 2 * a), lambda i: (i, 0, 0)),
            pl.BlockSpec((w, 2 * w), lambda i: (0, 0)),
        ],
        out_specs=pl.BlockSpec((nb, 2 * a, w), lambda i: (i, 0, 0)),
        out_shape=jax.ShapeDtypeStruct((b, 2 * a, w), F32),
        compiler_params=_params(("parallel",)),
        name="fourier_stage1",
    )(f.reshape(a, b * w), consts["t1"], consts["d"])
    mid = mid.reshape(b, 2, a, w)
    out = pl.pallas_call(
        _fft2_kernel,
        grid=(a // 8,),
        in_specs=[
            pl.BlockSpec((b, None, 8, w), lambda i: (0, 0, i, 0)),
            pl.BlockSpec((b, None, 8, w), lambda i: (0, 1, i, 0)),
            pl.BlockSpec((8 * b, 16 * b), lambda i: (0, 0)),
        ],
        out_specs=pl.BlockSpec((b, 8, w), lambda i: (0, i, 0)),
        out_shape=jax.ShapeDtypeStruct((b, a, w), F32),
        compiler_params=_params(("parallel",)),
        name="fourier_stage2",
    )(mid, mid, consts["t2"])
    return out.reshape(n, w)


def _fourier_consts(n):
    scale = 1.0 / math.sqrt(n * FNET_GROUP_DIM)
    d = jnp.asarray(_channel_dft(scale), BF16)
    if n <= 512:
        return {"d": d, "t": jnp.asarray(_dft_tables(n), BF16)}
    return {"d": d, "t1": jnp.asarray(_stage1_tables(n), BF16), "t2": jnp.asarray(_stage2_table(n), BF16)}


def _fill_window(win_ref, prev_ref, cur_ref, next_ref, tm):
    i = pl.program_id(0)
    last = pl.num_programs(0) - 1
    win_ref[0:HALO, :] = jnp.where(i > 0, prev_ref[tm - HALO:tm, :], 0.0)
    win_ref[HALO:HALO + tm, :] = cur_ref[...]
    win_ref[HALO + tm:HALO + tm + HALO, :] = jnp.where(i < last, next_ref[0:HALO, :], 0.0)


def _depthwise(win, w_ref, taps, rows):
    first = HALO - (taps - 1) // 2
    n = win.shape[0]
    acc = None
    for r in range(8):
        offs = [o for o in range(first, first + taps) if o % 8 == r]
        if not offs:
            continue
        base = win if r == 0 else pltpu.roll(win, n - r, 0)
        for o in offs:
            term = base[o - r:o - r + rows] * w_ref[o - first:o - first + 1, :]
            acc = term if acc is None else acc + term
    return acc


def _merge_kernel(x_ref, mod_ref, g_ref, attn_ref, fn_ref, hp_ref, hc_ref, hn_ref, sp_ref, sc_ref, sn_ref,
                  gb_ref, wg_ref, bg_ref, wa_ref, wf_ref, wc_ref, ws_ref, wo_ref,
                  dww_ref, dwb_ref, lng_ref, lnb_ref, scw_ref, o_ref, hwin_ref, swin_ref):
    tm, d = x_ref.shape
    _fill_window(hwin_ref, hp_ref, hc_ref, hn_ref, tm)
    _fill_window(swin_ref, sp_ref, sc_ref, sn_ref, tm)
    sub = tm // MERGE_SPLIT
    for part in range(MERGE_SPLIT):
        r0 = part * sub
        rows = slice(r0, r0 + sub)
        x = x_ref[rows, :]
        u = _modnorm(x, g_ref[...], mod_ref[0:1, :], mod_ref[1:2, :]).astype(BF16)

        c = _depthwise(hwin_ref[r0:r0 + sub + 2 * HALO, :], dww_ref, CONV_KERNEL, sub) + dwb_ref[...]
        mu = jnp.mean(c, axis=-1, keepdims=True)
        cc = c - mu
        var = jnp.mean(cc * cc, axis=-1, keepdims=True)
        conf = _silu((cc * lax.rsqrt(var + EPS)) * lng_ref[...] + lnb_ref[...])
        short = gb_ref[rows, :] * _depthwise(swin_ref[r0:r0 + sub + 2 * HALO, :], scw_ref, SC_KERNEL, sub)

        branches = (
            (attn_ref[rows, :], wa_ref),
            (fn_ref[rows, :].astype(BF16), wf_ref),
            (conf.astype(BF16), wc_ref),
            (short.astype(BF16), ws_ref),
        )
        merged = None
        for i, (val, w_ref) in enumerate(branches):
            cols = slice(i * d, (i + 1) * d)
            gate = jax.nn.sigmoid(_dot(u, wg_ref[:, cols]) + bg_ref[:, cols])
            term = gate * _dot(val, w_ref[...])
            merged = term if merged is None else merged + term
        o_ref[rows, :] = x + mod_ref[2:3, :] * _dot(merged.astype(BF16), wo_ref[...])


def _merge_call(x, mod3, g, attn, fn, hglu, sprod, gb, lw, *, tm_target):
    n, d = x.shape
    tm = _tile(n, tm_target)
    nt = n // tm
    row = lambda i: (i, 0)
    prev = lambda i: (jnp.maximum(i - 1, 0), 0)
    nxt = lambda i: (jnp.minimum(i + 1, nt - 1), 0)
    fix = lambda i: (0, 0)
    w = CONV_WIDTH

    def whole(arr):
        return pl.BlockSpec(arr.shape, fix, pipeline_mode=pl.Buffered(1))

    weights = (lw["wg"], lw["bg"], lw["wa"], lw["wf"], lw["wc"], lw["ws"], lw["wo"],
               lw["dww"], lw["dwb"], lw["lng"], lw["lnb"], lw["scw"])
    return pl.pallas_call(
        _merge_kernel,
        grid=(nt,),
        in_specs=[
            pl.BlockSpec((tm, d), row),
            pl.BlockSpec((3, d), fix),
            pl.BlockSpec((1, d), fix),
            pl.BlockSpec((tm, Q_W), row),
            pl.BlockSpec((tm, FNET_WIDTH), row),
            pl.BlockSpec((tm, w), prev), pl.BlockSpec((tm, w), row), pl.BlockSpec((tm, w), nxt),
            pl.BlockSpec((tm, w), prev), pl.BlockSpec((tm, w), row), pl.BlockSpec((tm, w), nxt),
            pl.BlockSpec((tm, w), row),
        ] + [whole(a) for a in weights],
        out_specs=pl.BlockSpec((tm, d), row),
        out_shape=jax.ShapeDtypeStruct((n, d), F32),
        scratch_shapes=[pltpu.VMEM((tm + 2 * HALO, w), F32), pltpu.VMEM((tm + 2 * HALO, w), F32)],
        compiler_params=_params(("arbitrary",)),
        name="mix_merge",
    )(x, mod3, g, attn, fn, hglu, hglu, hglu, sprod, sprod, sprod, gb, *weights)


def _rope_tables_t(n_tokens):
    n_rows = n_tokens // GRID_W
    row = jnp.broadcast_to(jnp.arange(n_rows)[:, None], (n_rows, GRID_W)).reshape(-1)
    col = jnp.broadcast_to(jnp.arange(GRID_W)[None, :], (n_rows, GRID_W)).reshape(-1)
    axis_dim = HEAD_DIM // 2
    inv_freq = ROPE_THETA ** (-jnp.arange(0, axis_dim, 2, dtype=F32) / axis_dim)
    pos = jnp.stack([row, col], axis=-1).astype(F32)
    ang = pos[:, :, None] * inv_freq
    ang_t = ang.reshape(n_tokens, axis_dim).T
    return jnp.cos(ang_t), jnp.sin(ang_t)


def kernel(x, c, ctx, c_ctx, w_ada, b_ada, norm_g, ffn_w13, ffn_w2, w_in, b_gate, q_norm_g, k_norm_g,
           w_attn_out, w_fnet, conv_dw_w, conv_dw_b, conv_ln_g, conv_ln_b, w_conv_out, sc_conv_w,
           w_sc_out, w_o, final_norm_g):
    batch, seq, d = x.shape
    assert batch == 1 and d == D_MODEL
    n_ctx = ctx.shape[1]
    depth = w_ada.shape[0]

    xs = x[0]
    cs = ctx[0]
    cos_x, sin_x = _rope_tables_t(seq)
    cos_c = jnp.ones((HEAD_DIM // 2, n_ctx), F32)
    sin_c = jnp.zeros((HEAD_DIM // 2, n_ctx), F32)
    fc_x = _fourier_consts(seq)
    fc_c = _fourier_consts(n_ctx)

    cc = jnp.zeros((8, d), F32).at[0].set(c[0]).at[1].set(c_ctx)
    mod = _mod_call(cc, w_ada, b_ada)
    mod = mod.reshape(depth, 8, N_MOD, d)

    w13 = ffn_w13.astype(BF16)
    w2 = ffn_w2.astype(BF16)
    wqkv_t = jnp.swapaxes(w_in[:, :, :QKV_W], 1, 2).astype(BF16)
    wrest = w_in[:, :, QKV_W:QKV_W + REST_W].astype(BF16)
    wg = w_in[:, :, QKV_W + REST_W:].astype(BF16)
    fg = final_norm_g.reshape(1, d)

    for l in range(depth):
        last = l == depth - 1
        mod_x, mod_c = mod[l, 0], mod[l, 1]
        lw = {
            "wg": wg[l], "bg": b_gate[l].reshape(1, -1),
            "wa": w_attn_out[l].astype(BF16), "wf": w_fnet[l].astype(BF16),
            "wc": w_conv_out[l].astype(BF16), "ws": w_sc_out[l].astype(BF16), "wo": w_o[l].astype(BF16),
            "dww": conv_dw_w[l], "dwb": conv_dw_b[l].reshape(1, -1),
            "lng": conv_ln_g[l].reshape(1, -1), "lnb": conv_ln_b[l].reshape(1, -1), "scw": sc_conv_w[l],
        }
        g0, g1, g2 = (norm_g[l, s].reshape(1, d) for s in range(3))
        qg = q_norm_g[l].reshape(HEAD_DIM, 1)
        kg = k_norm_g[l].reshape(HEAD_DIM, 1)

        xs = _ffn_call(xs, mod_x[0:3], g0, w13[l, 0], w2[l, 0], fg, final=False, tm_target=1024)
        cs = _ffn_call(cs, mod_c[0:3], g0, w13[l, 0], w2[l, 0], fg, final=False, tm_target=1024)

        q_x, k_x, v_x, f_x, h_x, s_x, gb_x = _inproj_call(
            xs, mod_x[3:5], g1, wqkv_t[l], wrest[l], qg, kg, cos_x, sin_x, tm_target=512)
        q_c, k_c, v_c, f_c, h_c, s_c, gb_c = _inproj_call(
            cs, mod_c[3:5], g1, wqkv_t[l], wrest[l], qg, kg, cos_c, sin_c, tm_target=512)

        k_all = jnp.concatenate([k_x, k_c], axis=0)
        v_all = jnp.concatenate([v_x, v_c], axis=1)
        bound = 8.0 * jnp.max(jnp.abs(q_norm_g[l])) * jnp.max(jnp.abs(k_norm_g[l]))
        safe = (bound <= SAFE_SCORE_BOUND).astype(jnp.int32).reshape(1)
        attn_x = _attn_call(safe, q_x, k_all, v_all, tq_target=1024, tk_target=3328)
        fn_x = _fourier_call(f_x, fc_x)
        xs = _merge_call(xs, mod_x[3:6], g1, attn_x, fn_x, h_x, s_x, gb_x, lw, tm_target=512)

        if not last:
            attn_c = _attn_call(safe, q_c, k_c, v_c, tq_target=1024, tk_target=1280)
            fn_c = _fourier_call(f_c, fc_c)
            cs = _merge_call(cs, mod_c[3:6], g1, attn_c, fn_c, h_c, s_c, gb_c, lw, tm_target=512)

        xs = _ffn_call(xs, mod_x[6:9], g2, w13[l, 1], w2[l, 1], fg, final=last, tm_target=1024)
        if not last:
            cs = _ffn_call(cs, mod_c[6:9], g2, w13[l, 1], w2[l, 1], fg, final=False, tm_target=1024)

    return xs[None]
```

```python
import functools
import math

import numpy as np
import jax
import jax.numpy as jnp
from jax import lax
from jax.experimental import pallas as pl
from jax.experimental.pallas import tpu as pltpu

F32 = jnp.float32
BF16 = jnp.bfloat16

D_MODEL = 1024
GRID_W = 64
N_HEADS = 8
N_KV_HEADS = 2
HEAD_DIM = 64
Q_GROUP = N_HEADS // N_KV_HEADS
Q_W = N_HEADS * HEAD_DIM
KV_W = N_KV_HEADS * HEAD_DIM
ROPE_THETA = 10000.0
FNET_GROUPS = 4
FNET_GROUP_DIM = 64
FNET_WIDTH = FNET_GROUPS * FNET_GROUP_DIM
CONV_WIDTH = 256
CONV_KERNEL = 31
SC_WIDTH = 256
SC_KERNEL = 3
N_BRANCHES = 4
D_FF = 2816
N_MOD = 9
EPS = 1e-6
QKV_W = Q_W + 2 * KV_W
REST_W = FNET_WIDTH + 2 * CONV_WIDTH + 3 * SC_WIDTH
V_ROWS = HEAD_DIM + 16
Q_SCALE = HEAD_DIM ** -0.5 * math.log2(math.e)
SAFE_SCORE_BOUND = 40.0
HALO = 16
FF_CHUNK = 256
MERGE_SPLIT = 2
V7X_VMEM_LIMIT_BYTES = 56 * 1024 * 1024
ADALN_COLS = 1152
FFN_ROWS = 1024
INPROJ_ROWS = 1024
MERGE_ROWS = 512
ATTN_QUERIES = 1024
ATTN_KEYS = 3328


def _tile(n, target):
    if n <= target:
        return n
    t = (target // 128) * 128
    while t >= 128:
        if n % t == 0:
            return t
        t -= 128
    return n


def _params(sem):
    return pltpu.CompilerParams(dimension_semantics=sem, vmem_limit_bytes=V7X_VMEM_LIMIT_BYTES)


def _dot(a, b):
    return jnp.dot(a, b, preferred_element_type=F32)


def _dot_nt(a, b):
    return lax.dot_general(a, b, (((1,), (1,)), ((), ())), preferred_element_type=F32)


def _modnorm(x, g, shift, scale):
    y = x * lax.rsqrt(jnp.mean(x * x, axis=-1, keepdims=True) + EPS)
    return (y * g) * (1.0 + scale) + shift


def _silu(x):
    return x * jax.nn.sigmoid(x)


def _mod_kernel(c_ref, w_ref, b_ref, o_ref):
    a = _silu(c_ref[...])
    o_ref[0] = _dot(a.astype(BF16), w_ref[0].astype(BF16)) + b_ref[0]


def _mod_call(cc, w_ada, b_ada):
    depth, d, n = w_ada.shape
    tn = _tile(n, ADALN_COLS)
    return pl.pallas_call(
        _mod_kernel,
        grid=(depth, n // tn),
        in_specs=[
            pl.BlockSpec((8, d), lambda l, j: (0, 0)),
            pl.BlockSpec((1, d, tn), lambda l, j: (l, 0, j)),
            pl.BlockSpec((1, 1, tn), lambda l, j: (l, 0, j)),
        ],
        out_specs=pl.BlockSpec((1, 8, tn), lambda l, j: (l, 0, j)),
        out_shape=jax.ShapeDtypeStruct((depth, 8, n), F32),
        compiler_params=_params(("parallel", "parallel")),
        name="adaln_mod",
    )(cc, w_ada, b_ada.reshape(depth, 1, n))


def _ffn_kernel(x_ref, mod_ref, g_ref, w13_ref, w2_ref, fg_ref, o_ref, *, final):
    x = x_ref[...]
    h = _modnorm(x, g_ref[...], mod_ref[0:1, :], mod_ref[1:2, :]).astype(BF16)
    acc = None
    for c in range(D_FF // FF_CHUNK):
        lo, hi = c * FF_CHUNK, (c + 1) * FF_CHUNK
        a = _dot(h, w13_ref[:, lo:hi])
        b = _dot(h, w13_ref[:, D_FF + lo:D_FF + hi])
        t = _dot((_silu(a) * b).astype(BF16), w2_ref[lo:hi, :])
        acc = t if acc is None else acc + t
    y = x + (0.5 * mod_ref[2:3, :]) * acc
    if final:
        y = (y * lax.rsqrt(jnp.mean(y * y, axis=-1, keepdims=True) + EPS)) * fg_ref[...]
    o_ref[...] = y


def _ffn_call(x, mod3, g, w13, w2, fg, *, final):
    n, d = x.shape
    tm = _tile(n, FFN_ROWS)
    fix = lambda i: (0, 0)
    return pl.pallas_call(
        functools.partial(_ffn_kernel, final=final),
        grid=(n // tm,),
        in_specs=[
            pl.BlockSpec((tm, d), lambda i: (i, 0)),
            pl.BlockSpec((3, d), fix),
            pl.BlockSpec((1, d), fix),
            pl.BlockSpec(w13.shape, fix, pipeline_mode=pl.Buffered(1)),
            pl.BlockSpec(w2.shape, fix, pipeline_mode=pl.Buffered(1)),
            pl.BlockSpec((1, d), fix),
        ],
        out_specs=pl.BlockSpec((tm, d), lambda i: (i, 0)),
        out_shape=jax.ShapeDtypeStruct((n, d), F32),
        compiler_params=_params(("parallel",)),
        name="ffn_half",
    )(x, mod3, g, w13, w2, fg)


def _head_norm_rope(t, gain, cos, sin, scale):
    ms = jnp.sum(t * t, axis=0, keepdims=True) * (1.0 / HEAD_DIM)
    t = (t * lax.rsqrt(ms + EPS)) * gain
    quarter = HEAD_DIM // 4
    parts = []
    for axis in range(2):
        x1 = t[2 * axis * quarter:(2 * axis + 1) * quarter]
        x2 = t[(2 * axis + 1) * quarter:(2 * axis + 2) * quarter]
        c = cos[axis * quarter:(axis + 1) * quarter]
        s = sin[axis * quarter:(axis + 1) * quarter]
        parts += [x1 * c - x2 * s, x1 * s + x2 * c]
    out = jnp.concatenate(parts, axis=0)
    return out * scale if scale != 1.0 else out


def _inproj_kernel(x_ref, mod_ref, g_ref, wqkv_ref, wrest_ref, qg_ref, kg_ref, cos_ref, sin_ref,
                   q_ref, k_ref, v_ref, f_ref, h_ref, s_ref, gb_ref):
    u = _modnorm(x_ref[...], g_ref[...], mod_ref[0:1, :], mod_ref[1:2, :]).astype(BF16)
    qkv = _dot_nt(wqkv_ref[...], u)
    cos = cos_ref[...]
    sin = sin_ref[...]
    zeros = jnp.zeros((HEAD_DIM, qkv.shape[1]), F32)
    for h in range(N_HEADS):
        t = _head_norm_rope(qkv[h * HEAD_DIM:(h + 1) * HEAD_DIM], qg_ref[...], cos, sin, Q_SCALE)
        blk = [t, zeros] if h // Q_GROUP == 0 else [zeros, t]
        q_ref[h * KV_W:(h + 1) * KV_W, :] = jnp.concatenate(blk, axis=0).astype(BF16)
    kt = [_head_norm_rope(qkv[Q_W + h * HEAD_DIM:Q_W + (h + 1) * HEAD_DIM], kg_ref[...], cos, sin, 1.0)
          for h in range(N_KV_HEADS)]
    k_ref[...] = jnp.concatenate(kt, axis=0).T.astype(BF16)
    ones = jnp.ones((V_ROWS - HEAD_DIM, qkv.shape[1]), F32)
    vt = []
    for h in range(N_KV_HEADS):
        vt += [qkv[Q_W + KV_W + h * HEAD_DIM:Q_W + KV_W + (h + 1) * HEAD_DIM], ones]
    v_ref[...] = jnp.concatenate(vt, axis=0).astype(BF16)

    r = _dot(u, wrest_ref[...])
    w = FNET_WIDTH
    f_ref[...] = r[:, 0:w]
    h_ref[...] = r[:, w:2 * w] * jax.nn.sigmoid(r[:, 2 * w:3 * w])
    gb_ref[...] = r[:, 3 * w:4 * w]
    s_ref[...] = r[:, 4 * w:5 * w] * r[:, 5 * w:6 * w]


def _inproj_call(x, mod2, g, wqkv_t, wrest, qg, kg, cos_t, sin_t):
    n, d = x.shape
    tm = _tile(n, INPROJ_ROWS)
    row = lambda i: (i, 0)
    col = lambda i: (0, i)
    fix = lambda i: (0, 0)
    w = FNET_WIDTH
    return pl.pallas_call(
        _inproj_kernel,
        grid=(n // tm,),
        in_specs=[
            pl.BlockSpec((tm, d), row),
            pl.BlockSpec((2, d), fix),
            pl.BlockSpec((1, d), fix),
            pl.BlockSpec((QKV_W, d), fix),
            pl.BlockSpec((d, REST_W), fix),
            pl.BlockSpec((HEAD_DIM, 1), fix),
            pl.BlockSpec((HEAD_DIM, 1), fix),
            pl.BlockSpec((HEAD_DIM // 2, tm), col),
            pl.BlockSpec((HEAD_DIM // 2, tm), col),
        ],
        out_specs=[
            pl.BlockSpec((N_HEADS * KV_W, tm), col),
            pl.BlockSpec((tm, KV_W), row),
            pl.BlockSpec((N_KV_HEADS * V_ROWS, tm), col),
            pl.BlockSpec((tm, w), row),
            pl.BlockSpec((tm, w), row),
            pl.BlockSpec((tm, w), row),
            pl.BlockSpec((tm, w), row),
        ],
        out_shape=[
            jax.ShapeDtypeStruct((N_HEADS * KV_W, n), BF16),
            jax.ShapeDtypeStruct((n, KV_W), BF16),
            jax.ShapeDtypeStruct((N_KV_HEADS * V_ROWS, n), BF16),
            jax.ShapeDtypeStruct((n, w), F32),
            jax.ShapeDtypeStruct((n, w), F32),
            jax.ShapeDtypeStruct((n, w), F32),
            jax.ShapeDtypeStruct((n, w), F32),
        ],
        compiler_params=_params(("parallel",)),
        name="in_proj",
    )(x, mod2, g, wqkv_t, wrest, qg, kg, cos_t, sin_t)


def _attn_kernel(safe_ref, q_ref, k_ref, v_ref, o_ref, m_ref, acc_ref):
    n_chunks = k_ref.shape[0]
    acc_ref[...] = jnp.zeros_like(acc_ref)

    def scores(j, h):
        return _dot(k_ref[j], q_ref[h * KV_W:(h + 1) * KV_W, :])

    @pl.when(safe_ref[0] != 0)
    def _():
        def chunk(j, carry):
            for h in range(Q_GROUP):
                rows = slice(h * V_ROWS, (h + 1) * V_ROWS)
                acc_ref[rows, :] += _dot(v_ref[j], jnp.exp2(scores(j, h)).astype(BF16))
            return carry

        lax.fori_loop(0, n_chunks, chunk, 0)

    @pl.when(safe_ref[0] == 0)
    def _():
        m_ref[...] = jnp.full_like(m_ref, -jnp.inf)

        def chunk(j, carry):
            for h in range(Q_GROUP):
                s = scores(j, h)
                m_old = m_ref[h:h + 1, :]
                m_new = jnp.maximum(m_old, jnp.max(s, axis=0, keepdims=True))
                rows = slice(h * V_ROWS, (h + 1) * V_ROWS)
                acc_ref[rows, :] = (jnp.exp2(m_old - m_new) * acc_ref[rows, :]
                                    + _dot(v_ref[j], jnp.exp2(s - m_new).astype(BF16)))
                m_ref[h:h + 1, :] = m_new
            return carry

        lax.fori_loop(0, n_chunks, chunk, 0)

    outs = []
    for h in range(Q_GROUP):
        r0 = h * V_ROWS
        outs.append(acc_ref[r0:r0 + HEAD_DIM, :] / acc_ref[r0 + HEAD_DIM:r0 + HEAD_DIM + 1, :])
    o_ref[...] = jnp.concatenate(outs, axis=0).T.astype(BF16)


def _attn_call(safe, q_t, k, v_t):
    nq = q_t.shape[1]
    nk = k.shape[0]
    tq = _tile(nq, ATTN_QUERIES)
    tk = _tile(nk, ATTN_KEYS)
    n_chunks = nk // tk
    gw = Q_GROUP * HEAD_DIM
    k3 = k.reshape(n_chunks, tk, KV_W)
    v4 = v_t.reshape(N_KV_HEADS, V_ROWS, n_chunks, tk).transpose(0, 2, 1, 3)
    return pl.pallas_call(
        _attn_kernel,
        grid=(N_KV_HEADS, nq // tq),
        in_specs=[
            pl.BlockSpec(memory_space=pltpu.SMEM),
            pl.BlockSpec((Q_GROUP * KV_W, tq), lambda g, i: (g, i)),
            pl.BlockSpec((n_chunks, tk, KV_W), lambda g, i: (0, 0, 0)),
            pl.BlockSpec((None, n_chunks, V_ROWS, tk), lambda g, i: (g, 0, 0, 0)),
        ],
        out_specs=pl.BlockSpec((tq, gw), lambda g, i: (i, g)),
        out_shape=jax.ShapeDtypeStruct((nq, Q_W), BF16),
        scratch_shapes=[pltpu.VMEM((8, tq), F32), pltpu.VMEM((Q_GROUP * V_ROWS, tq), F32)],
        compiler_params=_params(("parallel", "parallel")),
        name="attention",
    )(safe, q_t, k3, v4)


def _split_len(n):
    b = 1 << (int(math.log2(n)) // 2)
    return n // b, b


def _channel_dft(scale):
    c = np.arange(FNET_GROUP_DIM)
    ang = 2.0 * np.pi * np.outer(c, c) / FNET_GROUP_DIM
    eye = np.eye(FNET_GROUPS)
    return np.concatenate([np.kron(eye, np.cos(ang)), -np.kron(eye, np.sin(ang))], axis=1) * scale


def _stage1_tables(n):
    a, b = _split_len(n)
    c = np.arange(a)[None, :, None]
    pos = b * np.arange(a)[None, None, :] + np.arange(b)[:, None, None]
    ang = 2.0 * np.pi * ((c * pos) % n) / n
    co, si = np.cos(ang), np.sin(ang)
    return np.concatenate([np.concatenate([co, si], axis=2), np.concatenate([-si, co], axis=2)], axis=1)


def _stage2_table(n):
    a, b = _split_len(n)
    ang = 2.0 * np.pi * np.outer(np.arange(b), np.arange(b)) / b
    eye = np.eye(8)
    return np.concatenate([np.kron(np.cos(ang), eye), np.kron(np.sin(ang), eye)], axis=1)


def _dft_tables(n):
    ang = 2.0 * np.pi * np.outer(np.arange(n), np.arange(n)) / n
    return np.concatenate([np.cos(ang), np.sin(ang)], axis=1)


def _fft1_kernel(f_ref, t_ref, d_ref, o_ref, *, nb):
    w = FNET_WIDTH
    for b in range(nb):
        z = _dot(f_ref[:, b * w:(b + 1) * w].astype(BF16), d_ref[...])
        zz = jnp.concatenate([z[:, :w], z[:, w:]], axis=0).astype(BF16)
        o_ref[b] = _dot(t_ref[b], zz)


def _fft2_kernel(re_ref, im_ref, t_ref, o_ref):
    nb, _, w = re_ref.shape
    st = jnp.concatenate([re_ref[...].reshape(nb * 8, w), im_ref[...].reshape(nb * 8, w)], axis=0)
    o_ref[...] = _dot(t_ref[...], st.astype(BF16)).reshape(nb, 8, w)


def _fft_small_kernel(f_ref, t_ref, d_ref, o_ref):
    w = FNET_WIDTH
    z = _dot(f_ref[...].astype(BF16), d_ref[...])
    zz = jnp.concatenate([z[:, :w], z[:, w:]], axis=0).astype(BF16)
    o_ref[...] = _dot(t_ref[...], zz)


def _fourier_call(f, consts):
    n, w = f.shape
    if "t1" not in consts:
        return pl.pallas_call(
            _fft_small_kernel,
            out_shape=jax.ShapeDtypeStruct((n, w), F32),
            compiler_params=pltpu.CompilerParams(vmem_limit_bytes=V7X_VMEM_LIMIT_BYTES),
            name="fourier_small",
        )(f, consts["t"], consts["d"])
    a, b = _split_len(n)
    nb = 8
    mid = pl.pallas_call(
        functools.partial(_fft1_kernel, nb=nb),
        grid=(b // nb,),
        in_specs=[
            pl.BlockSpec((a, nb * w), lambda i: (0, i)),
            pl.BlockSpec((nb, 2 * a, 2 * a), lambda i: (i, 0, 0)),
            pl.BlockSpec((w, 2 * w), lambda i: (0, 0)),
        ],
        out_specs=pl.BlockSpec((nb, 2 * a, w), lambda i: (i, 0, 0)),
        out_shape=jax.ShapeDtypeStruct((b, 2 * a, w), F32),
        compiler_params=_params(("parallel",)),
        name="fourier_stage1",
    )(f.reshape(a, b * w), consts["t1"], consts["d"])
    mid = mid.reshape(b, 2, a, w)
    out = pl.pallas_call(
        _fft2_kernel,
        grid=(a // 8,),
        in_specs=[
            pl.BlockSpec((b, None, 8, w), lambda i: (0, 0, i, 0)),
            pl.BlockSpec((b, None, 8, w), lambda i: (0, 1, i, 0)),
            pl.BlockSpec((8 * b, 16 * b), lambda i: (0, 0)),
        ],
        out_specs=pl.BlockSpec((b, 8, w), lambda i: (0, i, 0)),
        out_shape=jax.ShapeDtypeStruct((b, a, w), F32),
        compiler_params=_params(("parallel",)),
        name="fourier_stage2",
    )(mid, mid, consts["t2"])
    return out.reshape(n, w)


def _fourier_consts(n):
    scale = 1.0 / math.sqrt(n * FNET_GROUP_DIM)
    d = jnp.asarray(_channel_dft(scale), BF16)
    if n <= 512:
        return {"d": d, "t": jnp.asarray(_dft_tables(n), BF16)}
    return {"d": d, "t1": jnp.asarray(_stage1_tables(n), BF16), "t2": jnp.asarray(_stage2_table(n), BF16)}


def _fill_window(win_ref, prev_ref, cur_ref, next_ref, tm):
    i = pl.program_id(0)
    last = pl.num_programs(0) - 1
    win_ref[0:HALO, :] = jnp.where(i > 0, prev_ref[tm - HALO:tm, :], 0.0)
    win_ref[HALO:HALO + tm, :] = cur_ref[...]
    win_ref[HALO + tm:HALO + tm + HALO, :] = jnp.where(i < last, next_ref[0:HALO, :], 0.0)


def _depthwise(win, w_ref, taps, rows):
    first = HALO - (taps - 1) // 2
    n = win.shape[0]
    acc = None
    for r in range(8):
        offs = [o for o in range(first, first + taps) if o % 8 == r]
        if not offs:
            continue
        base = win if r == 0 else pltpu.roll(win, n - r, 0)
        for o in offs:
            term = base[o - r:o - r + rows] * w_ref[o - first:o - first + 1, :]
            acc = term if acc is None else acc + term
    return acc


def _merge_kernel(x_ref, mod_ref, g_ref, attn_ref, fn_ref, hp_ref, hc_ref, hn_ref, sp_ref, sc_ref, sn_ref,
                  gb_ref, wg_ref, bg_ref, wa_ref, wf_ref, wc_ref, ws_ref, wo_ref,
                  dww_ref, dwb_ref, lng_ref, lnb_ref, scw_ref, o_ref, hwin_ref, swin_ref):
    tm, d = x_ref.shape
    _fill_window(hwin_ref, hp_ref, hc_ref, hn_ref, tm)
    _fill_window(swin_ref, sp_ref, sc_ref, sn_ref, tm)
    sub = tm // MERGE_SPLIT
    for part in range(MERGE_SPLIT):
        r0 = part * sub
        rows = slice(r0, r0 + sub)
        x = x_ref[rows, :]
        u = _modnorm(x, g_ref[...], mod_ref[0:1, :], mod_ref[1:2, :]).astype(BF16)

        c = _depthwise(hwin_ref[r0:r0 + sub + 2 * HALO, :], dww_ref, CONV_KERNEL, sub) + dwb_ref[...]
        mu = jnp.mean(c, axis=-1, keepdims=True)
        cc = c - mu
        var = jnp.mean(cc * cc, axis=-1, keepdims=True)
        conf = _silu((cc * lax.rsqrt(var + EPS)) * lng_ref[...] + lnb_ref[...])
        short = gb_ref[rows, :] * _depthwise(swin_ref[r0:r0 + sub + 2 * HALO, :], scw_ref, SC_KERNEL, sub)

        branches = (
            (attn_ref[rows, :], wa_ref),
            (fn_ref[rows, :].astype(BF16), wf_ref),
            (conf.astype(BF16), wc_ref),
            (short.astype(BF16), ws_ref),
        )
        merged = None
        for i, (val, w_ref) in enumerate(branches):
            cols = slice(i * d, (i + 1) * d)
            gate = jax.nn.sigmoid(_dot(u, wg_ref[:, cols]) + bg_ref[:, cols])
            term = gate * _dot(val, w_ref[...])
            merged = term if merged is None else merged + term
        o_ref[rows, :] = x + mod_ref[2:3, :] * _dot(merged.astype(BF16), wo_ref[...])


def _merge_call(x, mod3, g, attn, fn, hglu, sprod, gb, lw):
    n, d = x.shape
    tm = _tile(n, MERGE_ROWS)
    nt = n // tm
    row = lambda i: (i, 0)
    prev = lambda i: (jnp.maximum(i - 1, 0), 0)
    nxt = lambda i: (jnp.minimum(i + 1, nt - 1), 0)
    fix = lambda i: (0, 0)
    w = CONV_WIDTH

    def whole(arr):
        return pl.BlockSpec(arr.shape, fix, pipeline_mode=pl.Buffered(1))

    weights = (lw["wg"], lw["bg"], lw["wa"], lw["wf"], lw["wc"], lw["ws"], lw["wo"],
               lw["dww"], lw["dwb"], lw["lng"], lw["lnb"], lw["scw"])
    return pl.pallas_call(
        _merge_kernel,
        grid=(nt,),
        in_specs=[
            pl.BlockSpec((tm, d), row),
            pl.BlockSpec((3, d), fix),
            pl.BlockSpec((1, d), fix),
            pl.BlockSpec((tm, Q_W), row),
            pl.BlockSpec((tm, FNET_WIDTH), row),
            pl.BlockSpec((tm, w), prev), pl.BlockSpec((tm, w), row), pl.BlockSpec((tm, w), nxt),
            pl.BlockSpec((tm, w), prev), pl.BlockSpec((tm, w), row), pl.BlockSpec((tm, w), nxt),
            pl.BlockSpec((tm, w), row),
        ] + [whole(a) for a in weights],
        out_specs=pl.BlockSpec((tm, d), row),
        out_shape=jax.ShapeDtypeStruct((n, d), F32),
        scratch_shapes=[pltpu.VMEM((tm + 2 * HALO, w), F32), pltpu.VMEM((tm + 2 * HALO, w), F32)],
        compiler_params=_params(("arbitrary",)),
        name="mix_merge",
    )(x, mod3, g, attn, fn, hglu, hglu, hglu, sprod, sprod, sprod, gb, *weights)


def _rope_tables_t(n_tokens):
    n_rows = n_tokens // GRID_W
    row = jnp.broadcast_to(jnp.arange(n_rows)[:, None], (n_rows, GRID_W)).reshape(-1)
    col = jnp.broadcast_to(jnp.arange(GRID_W)[None, :], (n_rows, GRID_W)).reshape(-1)
    axis_dim = HEAD_DIM // 2
    inv_freq = ROPE_THETA ** (-jnp.arange(0, axis_dim, 2, dtype=F32) / axis_dim)
    pos = jnp.stack([row, col], axis=-1).astype(F32)
    ang = pos[:, :, None] * inv_freq
    ang_t = ang.reshape(n_tokens, axis_dim).T
    return jnp.cos(ang_t), jnp.sin(ang_t)


def kernel(x, c, ctx, c_ctx, w_ada, b_ada, norm_g, ffn_w13, ffn_w2, w_in, b_gate, q_norm_g, k_norm_g,
           w_attn_out, w_fnet, conv_dw_w, conv_dw_b, conv_ln_g, conv_ln_b, w_conv_out, sc_conv_w,
           w_sc_out, w_o, final_norm_g):
    batch, seq, d = x.shape
    assert batch == 1 and d == D_MODEL
    n_ctx = ctx.shape[1]
    depth = w_ada.shape[0]

    xs = x[0]
    cs = ctx[0]
    cos_x, sin_x = _rope_tables_t(seq)
    cos_c = jnp.ones((HEAD_DIM // 2, n_ctx), F32)
    sin_c = jnp.zeros((HEAD_DIM // 2, n_ctx), F32)
    fc_x = _fourier_consts(seq)
    fc_c = _fourier_consts(n_ctx)

    cc = jnp.zeros((8, d), F32).at[0].set(c[0]).at[1].set(c_ctx)
    mod = _mod_call(cc, w_ada, b_ada)
    mod = mod.reshape(depth, 8, N_MOD, d)

    w13 = ffn_w13.astype(BF16)
    w2 = ffn_w2.astype(BF16)
    wqkv_t = jnp.swapaxes(w_in[:, :, :QKV_W], 1, 2).astype(BF16)
    wrest = w_in[:, :, QKV_W:QKV_W + REST_W].astype(BF16)
    wg = w_in[:, :, QKV_W + REST_W:].astype(BF16)
    fg = final_norm_g.reshape(1, d)

    for l in range(depth):
        last = l == depth - 1
        mod_x, mod_c = mod[l, 0], mod[l, 1]
        lw = {
            "wg": wg[l], "bg": b_gate[l].reshape(1, -1),
            "wa": w_attn_out[l].astype(BF16), "wf": w_fnet[l].astype(BF16),
            "wc": w_conv_out[l].astype(BF16), "ws": w_sc_out[l].astype(BF16), "wo": w_o[l].astype(BF16),
            "dww": conv_dw_w[l], "dwb": conv_dw_b[l].reshape(1, -1),
            "lng": conv_ln_g[l].reshape(1, -1), "lnb": conv_ln_b[l].reshape(1, -1), "scw": sc_conv_w[l],
        }
        g0, g1, g2 = (norm_g[l, s].reshape(1, d) for s in range(3))
        qg = q_norm_g[l].reshape(HEAD_DIM, 1)
        kg = k_norm_g[l].reshape(HEAD_DIM, 1)

        xs = _ffn_call(xs, mod_x[0:3], g0, w13[l, 0], w2[l, 0], fg, final=False)
        cs = _ffn_call(cs, mod_c[0:3], g0, w13[l, 0], w2[l, 0], fg, final=False)

        q_x, k_x, v_x, f_x, h_x, s_x, gb_x = _inproj_call(
            xs, mod_x[3:5], g1, wqkv_t[l], wrest[l], qg, kg, cos_x, sin_x)
        q_c, k_c, v_c, f_c, h_c, s_c, gb_c = _inproj_call(
            cs, mod_c[3:5], g1, wqkv_t[l], wrest[l], qg, kg, cos_c, sin_c)

        k_all = jnp.concatenate([k_x, k_c], axis=0)
        v_all = jnp.concatenate([v_x, v_c], axis=1)
        bound = 8.0 * jnp.max(jnp.abs(q_norm_g[l])) * jnp.max(jnp.abs(k_norm_g[l]))
        safe = (bound <= SAFE_SCORE_BOUND).astype(jnp.int32).reshape(1)
        attn_x = _attn_call(safe, q_x, k_all, v_all)
        fn_x = _fourier_call(f_x, fc_x)
        xs = _merge_call(xs, mod_x[3:6], g1, attn_x, fn_x, h_x, s_x, gb_x, lw)

        if not last:
            attn_c = _attn_call(safe, q_c, k_c, v_c)
            fn_c = _fourier_call(f_c, fc_c)
            cs = _merge_call(cs, mod_c[3:6], g1, attn_c, fn_c, h_c, s_c, gb_c, lw)

        xs = _ffn_call(xs, mod_x[6:9], g2, w13[l, 1], w2[l, 1], fg, final=last)
        if not last:
            cs = _ffn_call(cs, mod_c[6:9], g2, w13[l, 1], w2[l, 1], fg, final=False)

    return xs[None]
```

```python
import functools
import math

import numpy as np
import jax
import jax.numpy as jnp
from jax import lax
from jax.experimental import pallas as pl
from jax.experimental.pallas import tpu as pltpu

F32 = jnp.float32
BF16 = jnp.bfloat16

D_MODEL = 1024
GRID_W = 64
N_HEADS = 8
N_KV_HEADS = 2
HEAD_DIM = 64
Q_GROUP = N_HEADS // N_KV_HEADS
Q_W = N_HEADS * HEAD_DIM
KV_W = N_KV_HEADS * HEAD_DIM
ROPE_THETA = 10000.0
FNET_GROUPS = 4
FNET_GROUP_DIM = 64
FNET_WIDTH = FNET_GROUPS * FNET_GROUP_DIM
CONV_WIDTH = 256
CONV_KERNEL = 31
SC_WIDTH = 256
SC_KERNEL = 3
N_BRANCHES = 4
D_FF = 2816
N_MOD = 9
EPS = 1e-6
QKV_W = Q_W + 2 * KV_W
REST_W = FNET_WIDTH + 2 * CONV_WIDTH + 3 * SC_WIDTH
V_ROWS = HEAD_DIM + 16
Q_SCALE = HEAD_DIM ** -0.5 * math.log2(math.e)
SAFE_SCORE_BOUND = 40.0
HALO = 16
FF_CHUNK = 256
MERGE_SPLIT = 2
V7X_VMEM_LIMIT_BYTES = 56 * 1024 * 1024
ADALN_COLS = 1152
FFN_ROWS = 512
INPROJ_ROWS = 1024
MERGE_ROWS = 512
ATTN_QUERIES = 1024
ATTN_KEYS = 8192


def _tile(n, target):
    if n <= target:
        return n
    t = (target // 128) * 128
    while t >= 128:
        if n % t == 0:
            return t
        t -= 128
    return n


def _params(sem):
    return pltpu.CompilerParams(dimension_semantics=sem, vmem_limit_bytes=V7X_VMEM_LIMIT_BYTES)


def _dot(a, b):
    return jnp.dot(a, b, preferred_element_type=F32)


def _dot_nt(a, b):
    return lax.dot_general(a, b, (((1,), (1,)), ((), ())), preferred_element_type=F32)


def _modnorm(x, g, shift, scale):
    y = x * lax.rsqrt(jnp.mean(x * x, axis=-1, keepdims=True) + EPS)
    return (y * g) * (1.0 + scale) + shift


def _silu(x):
    return x * jax.nn.sigmoid(x)


def _mod_kernel(c_ref, w_ref, b_ref, o_ref):
    a = _silu(c_ref[...])
    o_ref[0] = _dot(a.astype(BF16), w_ref[0].astype(BF16)) + b_ref[0]


def _mod_call(cc, w_ada, b_ada):
    depth, d, n = w_ada.shape
    tn = _tile(n, ADALN_COLS)
    return pl.pallas_call(
        _mod_kernel,
        grid=(depth, n // tn),
        in_specs=[
            pl.BlockSpec((8, d), lambda l, j: (0, 0)),
            pl.BlockSpec((1, d, tn), lambda l, j: (l, 0, j)),
            pl.BlockSpec((1, 1, tn), lambda l, j: (l, 0, j)),
        ],
        out_specs=pl.BlockSpec((1, 8, tn), lambda l, j: (l, 0, j)),
        out_shape=jax.ShapeDtypeStruct((depth, 8, n), F32),
        compiler_params=_params(("parallel", "parallel")),
        name="adaln_mod",
    )(cc, w_ada, b_ada.reshape(depth, 1, n))


def _ffn_rows(x, mod_ref, g_ref, w13_ref, w2_ref):
    h = _modnorm(x, g_ref[...], mod_ref[0:1, :], mod_ref[1:2, :]).astype(BF16)
    acc = None
    for c in range(D_FF // FF_CHUNK):
        lo, hi = c * FF_CHUNK, (c + 1) * FF_CHUNK
        a = _dot(h, w13_ref[:, lo:hi])
        b = _dot(h, w13_ref[:, D_FF + lo:D_FF + hi])
        t = _dot((_silu(a) * b).astype(BF16), w2_ref[lo:hi, :])
        acc = t if acc is None else acc + t
    return x + (0.5 * mod_ref[2:3, :]) * acc


def _ffn_final_kernel(x_ref, mod_ref, g_ref, w13_ref, w2_ref, fg_ref, o_ref):
    y = _ffn_rows(x_ref[...], mod_ref, g_ref, w13_ref, w2_ref)
    o_ref[...] = (y * lax.rsqrt(jnp.mean(y * y, axis=-1, keepdims=True) + EPS)) * fg_ref[...]


def _ffn_pair_kernel(x_ref, c_ref, modx_ref, modc_ref, g_ref, w13_ref, w2_ref, ox_ref, oc_ref):
    i = pl.program_id(0)
    last = pl.num_programs(0) - 1

    @pl.when(i < last)
    def _():
        ox_ref[...] = _ffn_rows(x_ref[...], modx_ref, g_ref, w13_ref, w2_ref)

    @pl.when(i == last)
    def _():
        oc_ref[...] = _ffn_rows(c_ref[...], modc_ref, g_ref, w13_ref, w2_ref)


def _ffn_weight_specs(w13_all, w2_all, layer, slot):
    pick = lambda i: (layer, slot, 0, 0)
    return [pl.BlockSpec((None, None) + w13_all.shape[2:], pick, pipeline_mode=pl.Buffered(1)),
            pl.BlockSpec((None, None) + w2_all.shape[2:], pick, pipeline_mode=pl.Buffered(1))]


def _ffn_final_call(x, mod3, g, w13_all, w2_all, layer, slot, fg):
    n, d = x.shape
    tm = _tile(n, FFN_ROWS)
    fix = lambda i: (0, 0)
    return pl.pallas_call(
        _ffn_final_kernel,
        grid=(n // tm,),
        in_specs=[pl.BlockSpec((tm, d), lambda i: (i, 0)), pl.BlockSpec((3, d), fix), pl.BlockSpec((1, d), fix)]
        + _ffn_weight_specs(w13_all, w2_all, layer, slot) + [pl.BlockSpec((1, d), fix)],
        out_specs=pl.BlockSpec((tm, d), lambda i: (i, 0)),
        out_shape=jax.ShapeDtypeStruct((n, d), F32),
        compiler_params=_params(("parallel",)),
        name="ffn_half",
    )(x, mod3, g, w13_all, w2_all, fg)


def _ffn_pair_call(x, c, modx3, modc3, g, w13_all, w2_all, layer, slot):
    n, d = x.shape
    nc = c.shape[0]
    tm = _tile(n, FFN_ROWS)
    nt = n // tm
    fix = lambda i: (0, 0)
    lat = lambda i: (jnp.minimum(i, nt - 1), 0)
    return pl.pallas_call(
        _ffn_pair_kernel,
        grid=(nt + 1,),
        in_specs=[pl.BlockSpec((tm, d), lat), pl.BlockSpec((nc, d), fix),
                  pl.BlockSpec((3, d), fix), pl.BlockSpec((3, d), fix), pl.BlockSpec((1, d), fix)]
        + _ffn_weight_specs(w13_all, w2_all, layer, slot),
        out_specs=[pl.BlockSpec((tm, d), lat), pl.BlockSpec((nc, d), fix)],
        out_shape=[jax.ShapeDtypeStruct((n, d), F32), jax.ShapeDtypeStruct((nc, d), F32)],
        compiler_params=_params(("arbitrary",)),
        name="ffn_half_pair",
    )(x, c, modx3, modc3, g, w13_all, w2_all)


def _head_norm_rope(t, gain, cos, sin, scale):
    ms = jnp.sum(t * t, axis=0, keepdims=True) * (1.0 / HEAD_DIM)
    t = (t * lax.rsqrt(ms + EPS)) * gain
    quarter = HEAD_DIM // 4
    parts = []
    for axis in range(2):
        x1 = t[2 * axis * quarter:(2 * axis + 1) * quarter]
        x2 = t[(2 * axis + 1) * quarter:(2 * axis + 2) * quarter]
        c = cos[axis * quarter:(axis + 1) * quarter]
        s = sin[axis * quarter:(axis + 1) * quarter]
        parts += [x1 * c - x2 * s, x1 * s + x2 * c]
    out = jnp.concatenate(parts, axis=0)
    return out * scale if scale != 1.0 else out


def _inproj_kernel(x_ref, mod_ref, g_ref, wqkv_ref, wrest_ref, qg_ref, kg_ref, cos_ref, sin_ref,
                   q_ref, k_ref, v_ref, f_ref, h_ref, s_ref, gb_ref):
    u = _modnorm(x_ref[...], g_ref[...], mod_ref[0:1, :], mod_ref[1:2, :]).astype(BF16)
    qkv = _dot_nt(wqkv_ref[...], u)
    cos = cos_ref[...]
    sin = sin_ref[...]
    zeros = jnp.zeros((HEAD_DIM, qkv.shape[1]), F32)
    for h in range(N_HEADS):
        t = _head_norm_rope(qkv[h * HEAD_DIM:(h + 1) * HEAD_DIM], qg_ref[...], cos, sin, Q_SCALE)
        blk = [t, zeros] if h // Q_GROUP == 0 else [zeros, t]
        q_ref[h * KV_W:(h + 1) * KV_W, :] = jnp.concatenate(blk, axis=0).astype(BF16)
    kt = [_head_norm_rope(qkv[Q_W + h * HEAD_DIM:Q_W + (h + 1) * HEAD_DIM], kg_ref[...], cos, sin, 1.0)
          for h in range(N_KV_HEADS)]
    k_ref[...] = jnp.concatenate(kt, axis=0).T.astype(BF16)
    ones = jnp.ones((V_ROWS - HEAD_DIM, qkv.shape[1]), F32)
    vt = []
    for h in range(N_KV_HEADS):
        vt += [qkv[Q_W + KV_W + h * HEAD_DIM:Q_W + KV_W + (h + 1) * HEAD_DIM], ones]
    v_ref[...] = jnp.concatenate(vt, axis=0).astype(BF16)

    r = _dot(u, wrest_ref[...])
    w = FNET_WIDTH
    f_ref[...] = r[:, 0:w]
    h_ref[...] = r[:, w:2 * w] * jax.nn.sigmoid(r[:, 2 * w:3 * w])
    gb_ref[...] = r[:, 3 * w:4 * w]
    s_ref[...] = r[:, 4 * w:5 * w] * r[:, 5 * w:6 * w]


def _inproj_call(x, mod2, g, wqkv_t, wrest, qg, kg, cos_t, sin_t):
    n, d = x.shape
    tm = _tile(n, INPROJ_ROWS)
    row = lambda i: (i, 0)
    col = lambda i: (0, i)
    fix = lambda i: (0, 0)
    w = FNET_WIDTH
    return pl.pallas_call(
        _inproj_kernel,
        grid=(n // tm,),
        in_specs=[
            pl.BlockSpec((tm, d), row),
            pl.BlockSpec((2, d), fix),
            pl.BlockSpec((1, d), fix),
            pl.BlockSpec((QKV_W, d), fix),
            pl.BlockSpec((d, REST_W), fix),
            pl.BlockSpec((HEAD_DIM, 1), fix),
            pl.BlockSpec((HEAD_DIM, 1), fix),
            pl.BlockSpec((HEAD_DIM // 2, tm), col),
            pl.BlockSpec((HEAD_DIM // 2, tm), col),
        ],
        out_specs=[
            pl.BlockSpec((N_HEADS * KV_W, tm), col),
            pl.BlockSpec((tm, KV_W), row),
            pl.BlockSpec((N_KV_HEADS * V_ROWS, tm), col),
            pl.BlockSpec((tm, w), row),
            pl.BlockSpec((tm, w), row),
            pl.BlockSpec((tm, w), row),
            pl.BlockSpec((tm, w), row),
        ],
        out_shape=[
            jax.ShapeDtypeStruct((N_HEADS * KV_W, n), BF16),
            jax.ShapeDtypeStruct((n, KV_W), BF16),
            jax.ShapeDtypeStruct((N_KV_HEADS * V_ROWS, n), BF16),
            jax.ShapeDtypeStruct((n, w), F32),
            jax.ShapeDtypeStruct((n, w), F32),
            jax.ShapeDtypeStruct((n, w), F32),
            jax.ShapeDtypeStruct((n, w), F32),
        ],
        compiler_params=_params(("parallel",)),
        name="in_proj",
    )(x, mod2, g, wqkv_t, wrest, qg, kg, cos_t, sin_t)


def _attn_kernel(safe_ref, q_ref, *refs, tk, n_sources):
    k_ref, v_ref = refs[0], refs[1]
    tail = refs[2:4] if n_sources == 2 else None
    o_ref, m_ref, acc_ref = refs[2 * n_sources:]
    n_chunks = k_ref.shape[0] // tk
    acc_ref[...] = jnp.zeros_like(acc_ref)

    def chunk_of(j):
        start = pl.multiple_of(j * tk, tk)
        return k_ref[pl.ds(start, tk), :], v_ref[:, pl.ds(start, tk)]

    def add_fast(kb, vb):
        for h in range(Q_GROUP):
            rows = slice(h * V_ROWS, (h + 1) * V_ROWS)
            s = _dot(kb, q_ref[h * KV_W:(h + 1) * KV_W, :])
            acc_ref[rows, :] += _dot(vb, jnp.exp2(s).astype(BF16))

    def add_slow(kb, vb):
        for h in range(Q_GROUP):
            rows = slice(h * V_ROWS, (h + 1) * V_ROWS)
            s = _dot(kb, q_ref[h * KV_W:(h + 1) * KV_W, :])
            m_old = m_ref[h:h + 1, :]
            m_new = jnp.maximum(m_old, jnp.max(s, axis=0, keepdims=True))
            acc_ref[rows, :] = (jnp.exp2(m_old - m_new) * acc_ref[rows, :]
                                + _dot(vb, jnp.exp2(s - m_new).astype(BF16)))
            m_ref[h:h + 1, :] = m_new

    def sweep(add):
        def chunk(j, carry):
            add(*chunk_of(j))
            return carry

        lax.fori_loop(0, n_chunks, chunk, 0)
        if tail is not None:
            add(tail[0][...], tail[1][...])

    @pl.when(safe_ref[0] != 0)
    def _():
        sweep(add_fast)

    @pl.when(safe_ref[0] == 0)
    def _():
        m_ref[...] = jnp.full_like(m_ref, -jnp.inf)
        sweep(add_slow)

    outs = []
    for h in range(Q_GROUP):
        r0 = h * V_ROWS
        outs.append(acc_ref[r0:r0 + HEAD_DIM, :] / acc_ref[r0 + HEAD_DIM:r0 + HEAD_DIM + 1, :])
    o_ref[...] = jnp.concatenate(outs, axis=0).T.astype(BF16)


def _attn_call(safe, q_t, sources):
    nq = q_t.shape[1]
    tq = _tile(nq, ATTN_QUERIES)
    tk = _tile(sources[0][0].shape[0], ATTN_KEYS)
    gw = Q_GROUP * HEAD_DIM
    kv_specs, kv_args = [], []
    for k, v_t in sources:
        nk = k.shape[0]
        kv_specs += [pl.BlockSpec((nk, KV_W), lambda g, i: (0, 0)), pl.BlockSpec((V_ROWS, nk), lambda g, i: (g, 0))]
        kv_args += [k, v_t]
    return pl.pallas_call(
        functools.partial(_attn_kernel, tk=tk, n_sources=len(sources)),
        grid=(N_KV_HEADS, nq // tq),
        in_specs=[pl.BlockSpec(memory_space=pltpu.SMEM), pl.BlockSpec((Q_GROUP * KV_W, tq), lambda g, i: (g, i))]
        + kv_specs,
        out_specs=pl.BlockSpec((tq, gw), lambda g, i: (i, g)),
        out_shape=jax.ShapeDtypeStruct((nq, Q_W), BF16),
        scratch_shapes=[pltpu.VMEM((8, tq), F32), pltpu.VMEM((Q_GROUP * V_ROWS, tq), F32)],
        compiler_params=_params(("parallel", "parallel")),
        name="attention",
    )(safe, q_t, *kv_args)


def _split_len(n):
    b = 1 << (int(math.log2(n)) // 2)
    return n // b, b


def _channel_dft(scale):
    c = np.arange(FNET_GROUP_DIM)
    ang = 2.0 * np.pi * np.outer(c, c) / FNET_GROUP_DIM
    eye = np.eye(FNET_GROUPS)
    return np.concatenate([np.kron(eye, np.cos(ang)), -np.kron(eye, np.sin(ang))], axis=1) * scale


def _stage1_tables(n):
    a, b = _split_len(n)
    c = np.arange(a)[None, :, None]
    pos = b * np.arange(a)[None, None, :] + np.arange(b)[:, None, None]
    ang = 2.0 * np.pi * ((c * pos) % n) / n
    co, si = np.cos(ang), np.sin(ang)
    return np.concatenate([np.concatenate([co, si], axis=2), np.concatenate([-si, co], axis=2)], axis=1)


def _stage2_table(n):
    a, b = _split_len(n)
    ang = 2.0 * np.pi * np.outer(np.arange(b), np.arange(b)) / b
    eye = np.eye(8)
    return np.concatenate([np.kron(np.cos(ang), eye), np.kron(np.sin(ang), eye)], axis=1)


def _dft_tables(n):
    ang = 2.0 * np.pi * np.outer(np.arange(n), np.arange(n)) / n
    return np.concatenate([np.cos(ang), np.sin(ang)], axis=1)


def _fft1_kernel(f_ref, t_ref, d_ref, o_ref, *, nb):
    w = FNET_WIDTH
    for b in range(nb):
        z = _dot(f_ref[:, b * w:(b + 1) * w].astype(BF16), d_ref[...])
        zz = jnp.concatenate([z[:, :w], z[:, w:]], axis=0).astype(BF16)
        o_ref[b] = _dot(t_ref[b], zz)


def _fft2_kernel(re_ref, im_ref, t_ref, o_ref):
    nb, _, w = re_ref.shape
    st = jnp.concatenate([re_ref[...].reshape(nb * 8, w), im_ref[...].reshape(nb * 8, w)], axis=0)
    o_ref[...] = _dot(t_ref[...], st.astype(BF16)).reshape(nb, 8, w)


def _fft_small_kernel(f_ref, t_ref, d_ref, o_ref):
    w = FNET_WIDTH
    z = _dot(f_ref[...].astype(BF16), d_ref[...])
    zz = jnp.concatenate([z[:, :w], z[:, w:]], axis=0).astype(BF16)
    o_ref[...] = _dot(t_ref[...], zz)


def _fourier_call(f, consts):
    n, w = f.shape
    if "t1" not in consts:
        return pl.pallas_call(
            _fft_small_kernel,
            out_shape=jax.ShapeDtypeStruct((n, w), F32),
            compiler_params=pltpu.CompilerParams(vmem_limit_bytes=V7X_VMEM_LIMIT_BYTES),
            name="fourier_small",
        )(f, consts["t"], consts["d"])
    a, b = _split_len(n)
    nb = 8
    mid = pl.pallas_call(
        functools.partial(_fft1_kernel, nb=nb),
        grid=(b // nb,),
        in_specs=[
            pl.BlockSpec((a, nb * w), lambda i: (0, i)),
            pl.BlockSpec((nb, 2 * a, 2 * a), lambda i: (i, 0, 0)),
            pl.BlockSpec((w, 2 * w), lambda i: (0, 0)),
        ],
        out_specs=pl.BlockSpec((nb, 2 * a, w), lambda i: (i, 0, 0)),
        out_shape=jax.ShapeDtypeStruct((b, 2 * a, w), F32),
        compiler_params=_params(("parallel",)),
        name="fourier_stage1",
    )(f.reshape(a, b * w), consts["t1"], consts["d"])
    mid = mid.reshape(b, 2, a, w)
    out = pl.pallas_call(
        _fft2_kernel,
        grid=(a // 8,),
        in_specs=[
            pl.BlockSpec((b, None, 8, w), lambda i: (0, 0, i, 0)),
            pl.BlockSpec((b, None, 8, w), lambda i: (0, 1, i, 0)),
            pl.BlockSpec((8 * b, 16 * b), lambda i: (0, 0)),
        ],
        out_specs=pl.BlockSpec((b, 8, w), lambda i: (0, i, 0)),
        out_shape=jax.ShapeDtypeStruct((b, a, w), F32),
        compiler_params=_params(("parallel",)),
        name="fourier_stage2",
    )(mid, mid, consts["t2"])
    return out.reshape(n, w)


def _fourier_consts(n):
    scale = 1.0 / math.sqrt(n * FNET_GROUP_DIM)
    d = jnp.asarray(_channel_dft(scale), BF16)
    if n <= 512:
        return {"d": d, "t": jnp.asarray(_dft_tables(n), BF16)}
    return {"d": d, "t1": jnp.asarray(_stage1_tables(n), BF16), "t2": jnp.asarray(_stage2_table(n), BF16)}


def _fill_window(win_ref, prev_ref, cur_ref, next_ref, tm):
    i = pl.program_id(0)
    last = pl.num_programs(0) - 1
    win_ref[0:HALO, :] = jnp.where(i > 0, prev_ref[tm - HALO:tm, :], 0.0)
    win_ref[HALO:HALO + tm, :] = cur_ref[...]
    win_ref[HALO + tm:HALO + tm + HALO, :] = jnp.where(i < last, next_ref[0:HALO, :], 0.0)


def _depthwise(win, w_ref, taps, rows):
    first = HALO - (taps - 1) // 2
    n = win.shape[0]
    acc = None
    for r in range(8):
        offs = [o for o in range(first, first + taps) if o % 8 == r]
        if not offs:
            continue
        base = win if r == 0 else pltpu.roll(win, n - r, 0)
        for o in offs:
            term = base[o - r:o - r + rows] * w_ref[o - first:o - first + 1, :]
            acc = term if acc is None else acc + term
    return acc


def _merge_kernel(x_ref, mod_ref, g_ref, attn_ref, fn_ref, hp_ref, hc_ref, hn_ref, sp_ref, sc_ref, sn_ref,
                  gb_ref, wg_ref, bg_ref, wa_ref, wf_ref, wc_ref, ws_ref, wo_ref,
                  dww_ref, dwb_ref, lng_ref, lnb_ref, scw_ref, o_ref, hwin_ref, swin_ref):
    tm, d = x_ref.shape
    _fill_window(hwin_ref, hp_ref, hc_ref, hn_ref, tm)
    _fill_window(swin_ref, sp_ref, sc_ref, sn_ref, tm)
    sub = tm // MERGE_SPLIT
    for part in range(MERGE_SPLIT):
        r0 = part * sub
        rows = slice(r0, r0 + sub)
        x = x_ref[rows, :]
        u = _modnorm(x, g_ref[...], mod_ref[0:1, :], mod_ref[1:2, :]).astype(BF16)

        c = _depthwise(hwin_ref[r0:r0 + sub + 2 * HALO, :], dww_ref, CONV_KERNEL, sub) + dwb_ref[...]
        mu = jnp.mean(c, axis=-1, keepdims=True)
        cc = c - mu
        var = jnp.mean(cc * cc, axis=-1, keepdims=True)
        conf = _silu((cc * lax.rsqrt(var + EPS)) * lng_ref[...] + lnb_ref[...])
        short = gb_ref[rows, :] * _depthwise(swin_ref[r0:r0 + sub + 2 * HALO, :], scw_ref, SC_KERNEL, sub)

        branches = (
            (attn_ref[rows, :], wa_ref),
            (fn_ref[rows, :].astype(BF16), wf_ref),
            (conf.astype(BF16), wc_ref),
            (short.astype(BF16), ws_ref),
        )
        merged = None
        for i, (val, w_ref) in enumerate(branches):
            cols = slice(i * d, (i + 1) * d)
            gate = jax.nn.sigmoid(_dot(u, wg_ref[:, cols]) + bg_ref[:, cols])
            term = gate * _dot(val, w_ref[...])
            merged = term if merged is None else merged + term
        o_ref[rows, :] = x + mod_ref[2:3, :] * _dot(merged.astype(BF16), wo_ref[...])


def _merge_call(x, mod3, g, attn, fn, hglu, sprod, gb, lw):
    n, d = x.shape
    tm = _tile(n, MERGE_ROWS)
    nt = n // tm
    row = lambda i: (i, 0)
    prev = lambda i: (jnp.maximum(i - 1, 0), 0)
    nxt = lambda i: (jnp.minimum(i + 1, nt - 1), 0)
    fix = lambda i: (0, 0)
    w = CONV_WIDTH

    def whole(arr):
        return pl.BlockSpec(arr.shape, fix, pipeline_mode=pl.Buffered(1))

    weights = (lw["wg"], lw["bg"], lw["wa"], lw["wf"], lw["wc"], lw["ws"], lw["wo"],
               lw["dww"], lw["dwb"], lw["lng"], lw["lnb"], lw["scw"])
    return pl.pallas_call(
        _merge_kernel,
        grid=(nt,),
        in_specs=[
            pl.BlockSpec((tm, d), row),
            pl.BlockSpec((3, d), fix),
            pl.BlockSpec((1, d), fix),
            pl.BlockSpec((tm, Q_W), row),
            pl.BlockSpec((tm, FNET_WIDTH), row),
            pl.BlockSpec((tm, w), prev), pl.BlockSpec((tm, w), row), pl.BlockSpec((tm, w), nxt),
            pl.BlockSpec((tm, w), prev), pl.BlockSpec((tm, w), row), pl.BlockSpec((tm, w), nxt),
            pl.BlockSpec((tm, w), row),
        ] + [whole(a) for a in weights],
        out_specs=pl.BlockSpec((tm, d), row),
        out_shape=jax.ShapeDtypeStruct((n, d), F32),
        scratch_shapes=[pltpu.VMEM((tm + 2 * HALO, w), F32), pltpu.VMEM((tm + 2 * HALO, w), F32)],
        compiler_params=_params(("arbitrary",)),
        name="mix_merge",
    )(x, mod3, g, attn, fn, hglu, hglu, hglu, sprod, sprod, sprod, gb, *weights)


def _rope_tables_t(n_tokens):
    n_rows = n_tokens // GRID_W
    row = jnp.broadcast_to(jnp.arange(n_rows)[:, None], (n_rows, GRID_W)).reshape(-1)
    col = jnp.broadcast_to(jnp.arange(GRID_W)[None, :], (n_rows, GRID_W)).reshape(-1)
    axis_dim = HEAD_DIM // 2
    inv_freq = ROPE_THETA ** (-jnp.arange(0, axis_dim, 2, dtype=F32) / axis_dim)
    pos = jnp.stack([row, col], axis=-1).astype(F32)
    ang = pos[:, :, None] * inv_freq
    ang_t = ang.reshape(n_tokens, axis_dim).T
    return jnp.cos(ang_t), jnp.sin(ang_t)


def kernel(x, c, ctx, c_ctx, w_ada, b_ada, norm_g, ffn_w13, ffn_w2, w_in, b_gate, q_norm_g, k_norm_g,
           w_attn_out, w_fnet, conv_dw_w, conv_dw_b, conv_ln_g, conv_ln_b, w_conv_out, sc_conv_w,
           w_sc_out, w_o, final_norm_g):
    batch, seq, d = x.shape
    assert batch == 1 and d == D_MODEL
    n_ctx = ctx.shape[1]
    depth = w_ada.shape[0]

    xs = x[0]
    cs = ctx[0]
    cos_x, sin_x = _rope_tables_t(seq)
    cos_c = jnp.ones((HEAD_DIM // 2, n_ctx), F32)
    sin_c = jnp.zeros((HEAD_DIM // 2, n_ctx), F32)
    fc_x = _fourier_consts(seq)
    fc_c = _fourier_consts(n_ctx)

    cc = jnp.zeros((8, d), F32).at[0].set(c[0]).at[1].set(c_ctx)
    mod = _mod_call(cc, w_ada, b_ada)
    mod = mod.reshape(depth, 8, N_MOD, d)

    wqkv_t = jnp.swapaxes(w_in[:, :, :QKV_W], 1, 2).astype(BF16)
    wrest = w_in[:, :, QKV_W:QKV_W + REST_W].astype(BF16)
    wg = w_in[:, :, QKV_W + REST_W:].astype(BF16)
    fg = final_norm_g.reshape(1, d)

    for l in range(depth):
        last = l == depth - 1
        mod_x, mod_c = mod[l, 0], mod[l, 1]
        lw = {
            "wg": wg[l], "bg": b_gate[l].reshape(1, -1),
            "wa": w_attn_out[l].astype(BF16), "wf": w_fnet[l].astype(BF16),
            "wc": w_conv_out[l].astype(BF16), "ws": w_sc_out[l].astype(BF16), "wo": w_o[l].astype(BF16),
            "dww": conv_dw_w[l], "dwb": conv_dw_b[l].reshape(1, -1),
            "lng": conv_ln_g[l].reshape(1, -1), "lnb": conv_ln_b[l].reshape(1, -1), "scw": sc_conv_w[l],
        }
        g0, g1, g2 = (norm_g[l, s].reshape(1, d) for s in range(3))
        qg = q_norm_g[l].reshape(HEAD_DIM, 1)
        kg = k_norm_g[l].reshape(HEAD_DIM, 1)

        xs, cs = _ffn_pair_call(xs, cs, mod_x[0:3], mod_c[0:3], g0, ffn_w13, ffn_w2, l, 0)

        q_x, k_x, v_x, f_x, h_x, s_x, gb_x = _inproj_call(
            xs, mod_x[3:5], g1, wqkv_t[l], wrest[l], qg, kg, cos_x, sin_x)
        q_c, k_c, v_c, f_c, h_c, s_c, gb_c = _inproj_call(
            cs, mod_c[3:5], g1, wqkv_t[l], wrest[l], qg, kg, cos_c, sin_c)

        bound = 8.0 * jnp.max(jnp.abs(q_norm_g[l])) * jnp.max(jnp.abs(k_norm_g[l]))
        safe = (bound <= SAFE_SCORE_BOUND).astype(jnp.int32).reshape(1)
        attn_x = _attn_call(safe, q_x, [(k_x, v_x), (k_c, v_c)])
        fn_x = _fourier_call(f_x, fc_x)
        xs = _merge_call(xs, mod_x[3:6], g1, attn_x, fn_x, h_x, s_x, gb_x, lw)

        if not last:
            attn_c = _attn_call(safe, q_c, [(k_c, v_c)])
            fn_c = _fourier_call(f_c, fc_c)
            cs = _merge_call(cs, mod_c[3:6], g1, attn_c, fn_c, h_c, s_c, gb_c, lw)

        if last:
            xs = _ffn_final_call(xs, mod_x[6:9], g2, ffn_w13, ffn_w2, l, 1, fg)
        else:
            xs, cs = _ffn_pair_call(xs, cs, mod_x[6:9], mod_c[6:9], g2, ffn_w13, ffn_w2, l, 1)

    return xs[None]
```

```python
import functools
import math

import numpy as np
import jax
import jax.numpy as jnp
from jax import lax
from jax.experimental import pallas as pl
from jax.experimental.pallas import tpu as pltpu

F32 = jnp.float32
BF16 = jnp.bfloat16

D_MODEL = 1024
GRID_W = 64
N_HEADS = 8
N_KV_HEADS = 2
HEAD_DIM = 64
Q_GROUP = N_HEADS // N_KV_HEADS
Q_W = N_HEADS * HEAD_DIM
KV_W = N_KV_HEADS * HEAD_DIM
ROPE_THETA = 10000.0
FNET_GROUPS = 4
FNET_GROUP_DIM = 64
FNET_WIDTH = FNET_GROUPS * FNET_GROUP_DIM
CONV_WIDTH = 256
CONV_KERNEL = 31
SC_WIDTH = 256
SC_KERNEL = 3
N_BRANCHES = 4
D_FF = 2816
N_MOD = 9
EPS = 1e-6
QKV_W = Q_W + 2 * KV_W
REST_W = FNET_WIDTH + 2 * CONV_WIDTH + 3 * SC_WIDTH
V_ROWS = HEAD_DIM + 16
Q_SCALE = HEAD_DIM ** -0.5 * math.log2(math.e)
SAFE_SCORE_BOUND = 40.0
HALO = 16
FF_CHUNK = 256
MERGE_SPLIT = 2
V7X_VMEM_LIMIT_BYTES = 56 * 1024 * 1024
ADALN_COLS = 1152
FFN_ROWS = 512
INPROJ_ROWS = 1024
MERGE_ROWS = 512
ATTN_QUERIES = 1024
ATTN_KEYS = 8192


def _tile(n, target):
    if n <= target:
        return n
    t = (target // 128) * 128
    while t >= 128:
        if n % t == 0:
            return t
        t -= 128
    return n


def _params(sem):
    return pltpu.CompilerParams(dimension_semantics=sem, vmem_limit_bytes=V7X_VMEM_LIMIT_BYTES)


def _dot(a, b):
    return jnp.dot(a, b, preferred_element_type=F32)


def _dot_nt(a, b):
    return lax.dot_general(a, b, (((1,), (1,)), ((), ())), preferred_element_type=F32)


def _modnorm(x, g, shift, scale):
    y = x * lax.rsqrt(jnp.mean(x * x, axis=-1, keepdims=True) + EPS)
    return (y * g) * (1.0 + scale) + shift


def _silu(x):
    return x * jax.nn.sigmoid(x)


def _mod_kernel(c_ref, w_ref, b_ref, o_ref):
    a = _silu(c_ref[...])
    o_ref[0] = _dot(a.astype(BF16), w_ref[0].astype(BF16)) + b_ref[0]


def _mod_call(cc, w_ada, b_ada):
    depth, d, n = w_ada.shape
    tn = _tile(n, ADALN_COLS)
    return pl.pallas_call(
        _mod_kernel,
        grid=(depth, n // tn),
        in_specs=[
            pl.BlockSpec((8, d), lambda l, j: (0, 0)),
            pl.BlockSpec((1, d, tn), lambda l, j: (l, 0, j)),
            pl.BlockSpec((1, 1, tn), lambda l, j: (l, 0, j)),
        ],
        out_specs=pl.BlockSpec((1, 8, tn), lambda l, j: (l, 0, j)),
        out_shape=jax.ShapeDtypeStruct((depth, 8, n), F32),
        compiler_params=_params(("parallel", "parallel")),
        name="adaln_mod",
    )(cc, w_ada, b_ada.reshape(depth, 1, n))


def _ffn_rows(x, mod_ref, g_ref, w13_ref, w2_ref):
    h = _modnorm(x, g_ref[...], mod_ref[0:1, :], mod_ref[1:2, :]).astype(BF16)
    acc = None
    for c in range(D_FF // FF_CHUNK):
        lo, hi = c * FF_CHUNK, (c + 1) * FF_CHUNK
        a = _dot(h, w13_ref[:, lo:hi])
        b = _dot(h, w13_ref[:, D_FF + lo:D_FF + hi])
        t = _dot((_silu(a) * b).astype(BF16), w2_ref[lo:hi, :])
        acc = t if acc is None else acc + t
    return x + (0.5 * mod_ref[2:3, :]) * acc


def _ffn_final_kernel(x_ref, mod_ref, g_ref, w13_ref, w2_ref, fg_ref, o_ref):
    y = _ffn_rows(x_ref[...], mod_ref, g_ref, w13_ref, w2_ref)
    o_ref[...] = (y * lax.rsqrt(jnp.mean(y * y, axis=-1, keepdims=True) + EPS)) * fg_ref[...]


def _ffn_pair_kernel(x_ref, c_ref, modx_ref, modc_ref, g_ref, w13_ref, w2_ref, ox_ref, oc_ref):
    i = pl.program_id(0)
    last = pl.num_programs(0) - 1

    @pl.when(i < last)
    def _():
        ox_ref[...] = _ffn_rows(x_ref[...], modx_ref, g_ref, w13_ref, w2_ref)

    @pl.when(i == last)
    def _():
        oc_ref[...] = _ffn_rows(c_ref[...], modc_ref, g_ref, w13_ref, w2_ref)


def _ffn_weight_specs(w13_all, w2_all, layer, slot):
    pick = lambda i: (layer, slot, 0, 0)
    return [pl.BlockSpec((None, None) + w13_all.shape[2:], pick, pipeline_mode=pl.Buffered(1)),
            pl.BlockSpec((None, None) + w2_all.shape[2:], pick, pipeline_mode=pl.Buffered(1))]


def _ffn_final_call(x, mod3, g, w13_all, w2_all, layer, slot, fg):
    n, d = x.shape
    tm = _tile(n, FFN_ROWS)
    fix = lambda i: (0, 0)
    return pl.pallas_call(
        _ffn_final_kernel,
        grid=(n // tm,),
        in_specs=[pl.BlockSpec((tm, d), lambda i: (i, 0)), pl.BlockSpec((3, d), fix), pl.BlockSpec((1, d), fix)]
        + _ffn_weight_specs(w13_all, w2_all, layer, slot) + [pl.BlockSpec((1, d), fix)],
        out_specs=pl.BlockSpec((tm, d), lambda i: (i, 0)),
        out_shape=jax.ShapeDtypeStruct((n, d), F32),
        compiler_params=_params(("parallel",)),
        name="ffn_half",
    )(x, mod3, g, w13_all, w2_all, fg)


def _ffn_pair_call(x, c, modx3, modc3, g, w13_all, w2_all, layer, slot):
    n, d = x.shape
    nc = c.shape[0]
    tm = _tile(n, FFN_ROWS)
    nt = n // tm
    fix = lambda i: (0, 0)
    lat = lambda i: (jnp.minimum(i, nt - 1), 0)
    return pl.pallas_call(
        _ffn_pair_kernel,
        grid=(nt + 1,),
        in_specs=[pl.BlockSpec((tm, d), lat), pl.BlockSpec((nc, d), fix),
                  pl.BlockSpec((3, d), fix), pl.BlockSpec((3, d), fix), pl.BlockSpec((1, d), fix)]
        + _ffn_weight_specs(w13_all, w2_all, layer, slot),
        out_specs=[pl.BlockSpec((tm, d), lat), pl.BlockSpec((nc, d), fix)],
        out_shape=[jax.ShapeDtypeStruct((n, d), F32), jax.ShapeDtypeStruct((nc, d), F32)],
        compiler_params=_params(("arbitrary",)),
        name="ffn_half_pair",
    )(x, c, modx3, modc3, g, w13_all, w2_all)


def _head_norm_rope(t, gain, cos, sin, scale):
    ms = jnp.sum(t * t, axis=0, keepdims=True) * (1.0 / HEAD_DIM)
    t = (t * lax.rsqrt(ms + EPS)) * gain
    quarter = HEAD_DIM // 4
    parts = []
    for axis in range(2):
        x1 = t[2 * axis * quarter:(2 * axis + 1) * quarter]
        x2 = t[(2 * axis + 1) * quarter:(2 * axis + 2) * quarter]
        c = cos[axis * quarter:(axis + 1) * quarter]
        s = sin[axis * quarter:(axis + 1) * quarter]
        parts += [x1 * c - x2 * s, x1 * s + x2 * c]
    out = jnp.concatenate(parts, axis=0)
    return out * scale if scale != 1.0 else out


def _inproj_kernel(x_ref, mod_ref, g_ref, wqkv_ref, wrest_ref, qg_ref, kg_ref, cos_ref, sin_ref,
                   q_ref, k_ref, v_ref, f_ref, h_ref, s_ref, gb_ref):
    u = _modnorm(x_ref[...], g_ref[...], mod_ref[0:1, :], mod_ref[1:2, :]).astype(BF16)
    qkv = _dot_nt(wqkv_ref[...], u)
    cos = cos_ref[...]
    sin = sin_ref[...]
    zeros = jnp.zeros((HEAD_DIM, qkv.shape[1]), F32)
    for h in range(N_HEADS):
        t = _head_norm_rope(qkv[h * HEAD_DIM:(h + 1) * HEAD_DIM], qg_ref[...], cos, sin, Q_SCALE)
        blk = [t, zeros] if h // Q_GROUP == 0 else [zeros, t]
        q_ref[h * KV_W:(h + 1) * KV_W, :] = jnp.concatenate(blk, axis=0).astype(BF16)
    kt = [_head_norm_rope(qkv[Q_W + h * HEAD_DIM:Q_W + (h + 1) * HEAD_DIM], kg_ref[...], cos, sin, 1.0)
          for h in range(N_KV_HEADS)]
    k_ref[...] = jnp.concatenate(kt, axis=0).T.astype(BF16)
    ones = jnp.ones((V_ROWS - HEAD_DIM, qkv.shape[1]), F32)
    vt = []
    for h in range(N_KV_HEADS):
        vt += [qkv[Q_W + KV_W + h * HEAD_DIM:Q_W + KV_W + (h + 1) * HEAD_DIM], ones]
    v_ref[...] = jnp.concatenate(vt, axis=0).astype(BF16)

    r = _dot(u, wrest_ref[...])
    w = FNET_WIDTH
    f_ref[...] = r[:, 0:w]
    h_ref[...] = r[:, w:2 * w] * jax.nn.sigmoid(r[:, 2 * w:3 * w])
    gb_ref[...] = r[:, 3 * w:4 * w]
    s_ref[...] = r[:, 4 * w:5 * w] * r[:, 5 * w:6 * w]


def _inproj_call(x, mod2, g, wqkv_t, wrest, qg, kg, cos_t, sin_t):
    n, d = x.shape
    tm = _tile(n, INPROJ_ROWS)
    row = lambda i: (i, 0)
    col = lambda i: (0, i)
    fix = lambda i: (0, 0)
    w = FNET_WIDTH
    return pl.pallas_call(
        _inproj_kernel,
        grid=(n // tm,),
        in_specs=[
            pl.BlockSpec((tm, d), row),
            pl.BlockSpec((2, d), fix),
            pl.BlockSpec((1, d), fix),
            pl.BlockSpec((QKV_W, d), fix),
            pl.BlockSpec((d, REST_W), fix),
            pl.BlockSpec((HEAD_DIM, 1), fix),
            pl.BlockSpec((HEAD_DIM, 1), fix),
            pl.BlockSpec((HEAD_DIM // 2, tm), col),
            pl.BlockSpec((HEAD_DIM // 2, tm), col),
        ],
        out_specs=[
            pl.BlockSpec((N_HEADS * KV_W, tm), col),
            pl.BlockSpec((tm, KV_W), row),
            pl.BlockSpec((N_KV_HEADS * V_ROWS, tm), col),
            pl.BlockSpec((tm, w), row),
            pl.BlockSpec((tm, w), row),
            pl.BlockSpec((tm, w), row),
            pl.BlockSpec((tm, w), row),
        ],
        out_shape=[
            jax.ShapeDtypeStruct((N_HEADS * KV_W, n), BF16),
            jax.ShapeDtypeStruct((n, KV_W), BF16),
            jax.ShapeDtypeStruct((N_KV_HEADS * V_ROWS, n), BF16),
            jax.ShapeDtypeStruct((n, w), F32),
            jax.ShapeDtypeStruct((n, w), F32),
            jax.ShapeDtypeStruct((n, w), F32),
            jax.ShapeDtypeStruct((n, w), F32),
        ],
        compiler_params=_params(("parallel",)),
        name="in_proj",
    )(x, mod2, g, wqkv_t, wrest, qg, kg, cos_t, sin_t)


def _attn_kernel(safe_ref, q_ref, *refs, tk, n_sources):
    k_ref, v_ref = refs[0], refs[1]
    tail = refs[2:4] if n_sources == 2 else None
    o_ref, m_ref, acc_ref = refs[2 * n_sources:]
    n_chunks = k_ref.shape[0] // tk
    acc_ref[...] = jnp.zeros_like(acc_ref)

    def chunk_of(j):
        start = pl.multiple_of(j * tk, tk)
        return k_ref[pl.ds(start, tk), :], v_ref[:, pl.ds(start, tk)]

    def add_fast(kb, vb):
        for h in range(Q_GROUP):
            rows = slice(h * V_ROWS, (h + 1) * V_ROWS)
            s = _dot(kb, q_ref[h * KV_W:(h + 1) * KV_W, :])
            acc_ref[rows, :] += _dot(vb, jnp.exp2(s).astype(BF16))

    def add_slow(kb, vb):
        for h in range(Q_GROUP):
            rows = slice(h * V_ROWS, (h + 1) * V_ROWS)
            s = _dot(kb, q_ref[h * KV_W:(h + 1) * KV_W, :])
            m_old = m_ref[h:h + 1, :]
            m_new = jnp.maximum(m_old, jnp.max(s, axis=0, keepdims=True))
            acc_ref[rows, :] = (jnp.exp2(m_old - m_new) * acc_ref[rows, :]
                                + _dot(vb, jnp.exp2(s - m_new).astype(BF16)))
            m_ref[h:h + 1, :] = m_new

    def sweep(add):
        def chunk(j, carry):
            add(*chunk_of(j))
            return carry

        lax.fori_loop(0, n_chunks, chunk, 0)
        if tail is not None:
            add(tail[0][...], tail[1][...])

    @pl.when(safe_ref[0] != 0)
    def _():
        sweep(add_fast)

    @pl.when(safe_ref[0] == 0)
    def _():
        m_ref[...] = jnp.full_like(m_ref, -jnp.inf)
        sweep(add_slow)

    outs = []
    for h in range(Q_GROUP):
        r0 = h * V_ROWS
        outs.append(acc_ref[r0:r0 + HEAD_DIM, :] / acc_ref[r0 + HEAD_DIM:r0 + HEAD_DIM + 1, :])
    o_ref[...] = jnp.concatenate(outs, axis=0).T.astype(BF16)


def _attn_call(safe, q_t, sources):
    nq = q_t.shape[1]
    tq = _tile(nq, ATTN_QUERIES)
    tk = _tile(sources[0][0].shape[0], ATTN_KEYS)
    gw = Q_GROUP * HEAD_DIM
    kv_specs, kv_args = [], []
    for k, v_t in sources:
        nk = k.shape[0]
        kv_specs += [pl.BlockSpec((nk, KV_W), lambda g, i: (0, 0)), pl.BlockSpec((V_ROWS, nk), lambda g, i: (g, 0))]
        kv_args += [k, v_t]
    return pl.pallas_call(
        functools.partial(_attn_kernel, tk=tk, n_sources=len(sources)),
        grid=(N_KV_HEADS, nq // tq),
        in_specs=[pl.BlockSpec(memory_space=pltpu.SMEM), pl.BlockSpec((Q_GROUP * KV_W, tq), lambda g, i: (g, i))]
        + kv_specs,
        out_specs=pl.BlockSpec((tq, gw), lambda g, i: (i, g)),
        out_shape=jax.ShapeDtypeStruct((nq, Q_W), BF16),
        scratch_shapes=[pltpu.VMEM((8, tq), F32), pltpu.VMEM((Q_GROUP * V_ROWS, tq), F32)],
        compiler_params=_params(("parallel", "parallel")),
        name="attention",
    )(safe, q_t, *kv_args)


def _split_len(n):
    b = 1 << (int(math.log2(n)) // 2)
    return n // b, b


def _channel_dft(scale):
    c = np.arange(FNET_GROUP_DIM)
    ang = 2.0 * np.pi * np.outer(c, c) / FNET_GROUP_DIM
    eye = np.eye(FNET_GROUPS)
    return np.concatenate([np.kron(eye, np.cos(ang)), -np.kron(eye, np.sin(ang))], axis=1) * scale


def _stage1_tables(n):
    a, b = _split_len(n)
    c = np.arange(a)[None, :, None]
    pos = b * np.arange(a)[None, None, :] + np.arange(b)[:, None, None]
    ang = 2.0 * np.pi * ((c * pos) % n) / n
    co, si = np.cos(ang), np.sin(ang)
    return np.concatenate([np.concatenate([co, si], axis=2), np.concatenate([-si, co], axis=2)], axis=1)


def _stage2_table(n):
    a, b = _split_len(n)
    ang = 2.0 * np.pi * np.outer(np.arange(b), np.arange(b)) / b
    eye = np.eye(8)
    return np.concatenate([np.kron(np.cos(ang), eye), np.kron(np.sin(ang), eye)], axis=1)


def _dft_tables(n):
    ang = 2.0 * np.pi * np.outer(np.arange(n), np.arange(n)) / n
    return np.concatenate([np.cos(ang), np.sin(ang)], axis=1)


def _fft_kernel(f_ref, t1_ref, d_ref, t2_ref, o_ref, mid_ref, *, nb, n1):
    i = pl.program_id(0)
    w = FNET_WIDTH
    a = f_ref.shape[0]

    @pl.when(i < n1)
    def _():
        for bl in range(nb):
            z = _dot(f_ref[:, bl * w:(bl + 1) * w].astype(BF16), d_ref[...])
            zz = jnp.concatenate([z[:, :w], z[:, w:]], axis=0).astype(BF16)
            mid_ref[i * nb + bl] = _dot(t1_ref[bl], zz).reshape(2, a, w)

    @pl.when(i >= n1)
    def _():
        nbt = mid_ref.shape[0]
        c0 = pl.multiple_of((i - n1) * 8, 8)
        re = mid_ref[:, 0, pl.ds(c0, 8), :].reshape(nbt * 8, w)
        im = mid_ref[:, 1, pl.ds(c0, 8), :].reshape(nbt * 8, w)
        st = jnp.concatenate([re, im], axis=0).astype(BF16)
        o_ref[...] = _dot(t2_ref[...], st).reshape(nbt, 8, w)


def _fft_small_kernel(f_ref, t_ref, d_ref, o_ref):
    w = FNET_WIDTH
    z = _dot(f_ref[...].astype(BF16), d_ref[...])
    zz = jnp.concatenate([z[:, :w], z[:, w:]], axis=0).astype(BF16)
    o_ref[...] = _dot(t_ref[...], zz)


def _fourier_call(f, consts):
    n, w = f.shape
    if "t1" not in consts:
        return pl.pallas_call(
            _fft_small_kernel,
            out_shape=jax.ShapeDtypeStruct((n, w), F32),
            compiler_params=pltpu.CompilerParams(vmem_limit_bytes=V7X_VMEM_LIMIT_BYTES),
            name="fourier_small",
        )(f, consts["t"], consts["d"])
    a, b = _split_len(n)
    nb = 8
    n1, n2 = b // nb, a // 8
    out = pl.pallas_call(
        functools.partial(_fft_kernel, nb=nb, n1=n1),
        grid=(n1 + n2,),
        in_specs=[
            pl.BlockSpec((a, nb * w), lambda i: (0, jnp.minimum(i, n1 - 1))),
            pl.BlockSpec((nb, 2 * a, 2 * a), lambda i: (jnp.minimum(i, n1 - 1), 0, 0)),
            pl.BlockSpec((w, 2 * w), lambda i: (0, 0)),
            pl.BlockSpec((8 * b, 16 * b), lambda i: (0, 0), pipeline_mode=pl.Buffered(1)),
        ],
        out_specs=pl.BlockSpec((b, 8, w), lambda i: (0, jnp.maximum(i - n1, 0), 0)),
        out_shape=jax.ShapeDtypeStruct((b, a, w), F32),
        scratch_shapes=[pltpu.VMEM((b, 2, a, w), F32)],
        compiler_params=_params(("arbitrary",)),
        name="fourier",
    )(f.reshape(a, b * w), consts["t1"], consts["d"], consts["t2"])
    return out.reshape(n, w)


def _fourier_consts(n):
    scale = 1.0 / math.sqrt(n * FNET_GROUP_DIM)
    d = jnp.asarray(_channel_dft(scale), BF16)
    if n <= 512:
        return {"d": d, "t": jnp.asarray(_dft_tables(n), BF16)}
    return {"d": d, "t1": jnp.asarray(_stage1_tables(n), BF16), "t2": jnp.asarray(_stage2_table(n), BF16)}


def _fill_window(win_ref, prev_ref, cur_ref, next_ref, tm):
    i = pl.program_id(0)
    last = pl.num_programs(0) - 1
    win_ref[0:HALO, :] = jnp.where(i > 0, prev_ref[tm - HALO:tm, :], 0.0)
    win_ref[HALO:HALO + tm, :] = cur_ref[...]
    win_ref[HALO + tm:HALO + tm + HALO, :] = jnp.where(i < last, next_ref[0:HALO, :], 0.0)


def _depthwise(win, w_ref, taps, rows):
    first = HALO - (taps - 1) // 2
    n = win.shape[0]
    acc = None
    for r in range(8):
        offs = [o for o in range(first, first + taps) if o % 8 == r]
        if not offs:
            continue
        base = win if r == 0 else pltpu.roll(win, n - r, 0)
        for o in offs:
            term = base[o - r:o - r + rows] * w_ref[o - first:o - first + 1, :]
            acc = term if acc is None else acc + term
    return acc


def _merge_kernel(x_ref, mod_ref, g_ref, attn_ref, fn_ref, hp_ref, hc_ref, hn_ref, sp_ref, sc_ref, sn_ref,
                  gb_ref, wg_ref, bg_ref, wa_ref, wf_ref, wc_ref, ws_ref, wo_ref,
                  dww_ref, dwb_ref, lng_ref, lnb_ref, scw_ref, o_ref, hwin_ref, swin_ref):
    tm, d = x_ref.shape
    _fill_window(hwin_ref, hp_ref, hc_ref, hn_ref, tm)
    _fill_window(swin_ref, sp_ref, sc_ref, sn_ref, tm)
    sub = tm // MERGE_SPLIT
    for part in range(MERGE_SPLIT):
        r0 = part * sub
        rows = slice(r0, r0 + sub)
        x = x_ref[rows, :]
        u = _modnorm(x, g_ref[...], mod_ref[0:1, :], mod_ref[1:2, :]).astype(BF16)

        c = _depthwise(hwin_ref[r0:r0 + sub + 2 * HALO, :], dww_ref, CONV_KERNEL, sub) + dwb_ref[...]
        mu = jnp.mean(c, axis=-1, keepdims=True)
        cc = c - mu
        var = jnp.mean(cc * cc, axis=-1, keepdims=True)
        conf = _silu((cc * lax.rsqrt(var + EPS)) * lng_ref[...] + lnb_ref[...])
        short = gb_ref[rows, :] * _depthwise(swin_ref[r0:r0 + sub + 2 * HALO, :], scw_ref, SC_KERNEL, sub)

        branches = (
            (attn_ref[rows, :], wa_ref),
            (fn_ref[rows, :].astype(BF16), wf_ref),
            (conf.astype(BF16), wc_ref),
            (short.astype(BF16), ws_ref),
        )
        merged = None
        for i, (val, w_ref) in enumerate(branches):
            cols = slice(i * d, (i + 1) * d)
            gate = jax.nn.sigmoid(_dot(u, wg_ref[:, cols]) + bg_ref[:, cols])
            term = gate * _dot(val, w_ref[...])
            merged = term if merged is None else merged + term
        o_ref[rows, :] = x + mod_ref[2:3, :] * _dot(merged.astype(BF16), wo_ref[...])


def _merge_call(x, mod3, g, attn, fn, hglu, sprod, gb, lw):
    n, d = x.shape
    tm = _tile(n, MERGE_ROWS)
    nt = n // tm
    row = lambda i: (i, 0)
    prev = lambda i: (jnp.maximum(i - 1, 0), 0)
    nxt = lambda i: (jnp.minimum(i + 1, nt - 1), 0)
    fix = lambda i: (0, 0)
    w = CONV_WIDTH

    def whole(arr):
        return pl.BlockSpec(arr.shape, fix, pipeline_mode=pl.Buffered(1))

    weights = (lw["wg"], lw["bg"], lw["wa"], lw["wf"], lw["wc"], lw["ws"], lw["wo"],
               lw["dww"], lw["dwb"], lw["lng"], lw["lnb"], lw["scw"])
    return pl.pallas_call(
        _merge_kernel,
        grid=(nt,),
        in_specs=[
            pl.BlockSpec((tm, d), row),
            pl.BlockSpec((3, d), fix),
            pl.BlockSpec((1, d), fix),
            pl.BlockSpec((tm, Q_W), row),
            pl.BlockSpec((tm, FNET_WIDTH), row),
            pl.BlockSpec((tm, w), prev), pl.BlockSpec((tm, w), row), pl.BlockSpec((tm, w), nxt),
            pl.BlockSpec((tm, w), prev), pl.BlockSpec((tm, w), row), pl.BlockSpec((tm, w), nxt),
            pl.BlockSpec((tm, w), row),
        ] + [whole(a) for a in weights],
        out_specs=pl.BlockSpec((tm, d), row),
        out_shape=jax.ShapeDtypeStruct((n, d), F32),
        scratch_shapes=[pltpu.VMEM((tm + 2 * HALO, w), F32), pltpu.VMEM((tm + 2 * HALO, w), F32)],
        compiler_params=_params(("arbitrary",)),
        name="mix_merge",
    )(x, mod3, g, attn, fn, hglu, hglu, hglu, sprod, sprod, sprod, gb, *weights)


def _rope_tables_t(n_tokens):
    n_rows = n_tokens // GRID_W
    row = jnp.broadcast_to(jnp.arange(n_rows)[:, None], (n_rows, GRID_W)).reshape(-1)
    col = jnp.broadcast_to(jnp.arange(GRID_W)[None, :], (n_rows, GRID_W)).reshape(-1)
    axis_dim = HEAD_DIM // 2
    inv_freq = ROPE_THETA ** (-jnp.arange(0, axis_dim, 2, dtype=F32) / axis_dim)
    pos = jnp.stack([row, col], axis=-1).astype(F32)
    ang = pos[:, :, None] * inv_freq
    ang_t = ang.reshape(n_tokens, axis_dim).T
    return jnp.cos(ang_t), jnp.sin(ang_t)


def kernel(x, c, ctx, c_ctx, w_ada, b_ada, norm_g, ffn_w13, ffn_w2, w_in, b_gate, q_norm_g, k_norm_g,
           w_attn_out, w_fnet, conv_dw_w, conv_dw_b, conv_ln_g, conv_ln_b, w_conv_out, sc_conv_w,
           w_sc_out, w_o, final_norm_g):
    batch, seq, d = x.shape
    assert batch == 1 and d == D_MODEL
    n_ctx = ctx.shape[1]
    depth = w_ada.shape[0]

    xs = x[0]
    cs = ctx[0]
    cos_x, sin_x = _rope_tables_t(seq)
    cos_c = jnp.ones((HEAD_DIM // 2, n_ctx), F32)
    sin_c = jnp.zeros((HEAD_DIM // 2, n_ctx), F32)
    fc_x = _fourier_consts(seq)
    fc_c = _fourier_consts(n_ctx)

    cc = jnp.zeros((8, d), F32).at[0].set(c[0]).at[1].set(c_ctx)
    mod = _mod_call(cc, w_ada, b_ada)
    mod = mod.reshape(depth, 8, N_MOD, d)

    wqkv_t = jnp.swapaxes(w_in[:, :, :QKV_W], 1, 2).astype(BF16)
    wrest = w_in[:, :, QKV_W:QKV_W + REST_W].astype(BF16)
    wg = w_in[:, :, QKV_W + REST_W:].astype(BF16)
    fg = final_norm_g.reshape(1, d)

    for l in range(depth):
        last = l == depth - 1
        mod_x, mod_c = mod[l, 0], mod[l, 1]
        lw = {
            "wg": wg[l], "bg": b_gate[l].reshape(1, -1),
            "wa": w_attn_out[l].astype(BF16), "wf": w_fnet[l].astype(BF16),
            "wc": w_conv_out[l].astype(BF16), "ws": w_sc_out[l].astype(BF16), "wo": w_o[l].astype(BF16),
            "dww": conv_dw_w[l], "dwb": conv_dw_b[l].reshape(1, -1),
            "lng": conv_ln_g[l].reshape(1, -1), "lnb": conv_ln_b[l].reshape(1, -1), "scw": sc_conv_w[l],
        }
        g0, g1, g2 = (norm_g[l, s].reshape(1, d) for s in range(3))
        qg = q_norm_g[l].reshape(HEAD_DIM, 1)
        kg = k_norm_g[l].reshape(HEAD_DIM, 1)

        xs, cs = _ffn_pair_call(xs, cs, mod_x[0:3], mod_c[0:3], g0, ffn_w13, ffn_w2, l, 0)

        q_x, k_x, v_x, f_x, h_x, s_x, gb_x = _inproj_call(
            xs, mod_x[3:5], g1, wqkv_t[l], wrest[l], qg, kg, cos_x, sin_x)
        q_c, k_c, v_c, f_c, h_c, s_c, gb_c = _inproj_call(
            cs, mod_c[3:5], g1, wqkv_t[l], wrest[l], qg, kg, cos_c, sin_c)

        bound = 8.0 * jnp.max(jnp.abs(q_norm_g[l])) * jnp.max(jnp.abs(k_norm_g[l]))
        safe = (bound <= SAFE_SCORE_BOUND).astype(jnp.int32).reshape(1)
        attn_x = _attn_call(safe, q_x, [(k_x, v_x), (k_c, v_c)])
        fn_x = _fourier_call(f_x, fc_x)
        xs = _merge_call(xs, mod_x[3:6], g1, attn_x, fn_x, h_x, s_x, gb_x, lw)

        if not last:
            attn_c = _attn_call(safe, q_c, [(k_c, v_c)])
            fn_c = _fourier_call(f_c, fc_c)
            cs = _merge_call(cs, mod_c[3:6], g1, attn_c, fn_c, h_c, s_c, gb_c, lw)

        if last:
            xs = _ffn_final_call(xs, mod_x[6:9], g2, ffn_w13, ffn_w2, l, 1, fg)
        else:
            xs, cs = _ffn_pair_call(xs, cs, mod_x[6:9], mod_c[6:9], g2, ffn_w13, ffn_w2, l, 1)

    return xs[None]
```

```python
import functools
import math

import numpy as np
import jax
import jax.numpy as jnp
from jax import lax
from jax.experimental import pallas as pl
from jax.experimental.pallas import tpu as pltpu

F32 = jnp.float32
BF16 = jnp.bfloat16

D_MODEL = 1024
GRID_W = 64
N_HEADS = 8
N_KV_HEADS = 2
HEAD_DIM = 64
Q_GROUP = N_HEADS // N_KV_HEADS
Q_W = N_HEADS * HEAD_DIM
KV_W = N_KV_HEADS * HEAD_DIM
ROPE_THETA = 10000.0
FNET_GROUPS = 4
FNET_GROUP_DIM = 64
FNET_WIDTH = FNET_GROUPS * FNET_GROUP_DIM
CONV_WIDTH = 256
CONV_KERNEL = 31
SC_WIDTH = 256
SC_KERNEL = 3
N_BRANCHES = 4
D_FF = 2816
N_MOD = 9
EPS = 1e-6
QKV_W = Q_W + 2 * KV_W
REST_W = FNET_WIDTH + 2 * CONV_WIDTH + 3 * SC_WIDTH
V_ROWS = HEAD_DIM + 16
Q_SCALE = HEAD_DIM ** -0.5 * math.log2(math.e)
SAFE_SCORE_BOUND = 40.0
HALO = 16
FF_CHUNK = 256
FOURIER_DIRECT_MAX = 512
MERGE_SPLIT = 2
V7X_VMEM_LIMIT_BYTES = 56 * 1024 * 1024
ADALN_COLS = 1152
FFN_ROWS = 512
INPROJ_ROWS = 1024
MERGE_ROWS = 512
ATTN_QUERIES = 1024
ATTN_KEYS = 8192


def _tile(n, target):
    if n <= target:
        return n
    t = (target // 128) * 128
    while t >= 128:
        if n % t == 0:
            return t
        t -= 128
    return n


def _params(sem):
    return pltpu.CompilerParams(dimension_semantics=sem, vmem_limit_bytes=V7X_VMEM_LIMIT_BYTES)


def _dot(a, b):
    return jnp.dot(a, b, preferred_element_type=F32)


def _dot_nt(a, b):
    return lax.dot_general(a, b, (((1,), (1,)), ((), ())), preferred_element_type=F32)


def _modnorm(x, g, shift, scale):
    y = x * lax.rsqrt(jnp.mean(x * x, axis=-1, keepdims=True) + EPS)
    return (y * g) * (1.0 + scale) + shift


def _silu(x):
    return x * jax.nn.sigmoid(x)


def _mod_kernel(c_ref, w_ref, b_ref, o_ref):
    a = _silu(c_ref[...])
    o_ref[0] = _dot(a.astype(BF16), w_ref[0].astype(BF16)) + b_ref[0]


def _mod_call(cc, w_ada, b_ada):
    depth, d, n = w_ada.shape
    tn = _tile(n, ADALN_COLS)
    return pl.pallas_call(
        _mod_kernel,
        grid=(depth, n // tn),
        in_specs=[
            pl.BlockSpec((8, d), lambda l, j: (0, 0)),
            pl.BlockSpec((1, d, tn), lambda l, j: (l, 0, j)),
            pl.BlockSpec((1, 1, tn), lambda l, j: (l, 0, j)),
        ],
        out_specs=pl.BlockSpec((1, 8, tn), lambda l, j: (l, 0, j)),
        out_shape=jax.ShapeDtypeStruct((depth, 8, n), F32),
        compiler_params=_params(("parallel", "parallel")),
        name="adaln_mod",
    )(cc, w_ada, b_ada.reshape(depth, 1, n))


def _ffn_rows(x, mod_ref, g_ref, w13_ref, w2_ref):
    h = _modnorm(x, g_ref[...], mod_ref[0:1, :], mod_ref[1:2, :]).astype(BF16)
    acc = None
    for c in range(D_FF // FF_CHUNK):
        lo, hi = c * FF_CHUNK, (c + 1) * FF_CHUNK
        a = _dot(h, w13_ref[:, lo:hi])
        b = _dot(h, w13_ref[:, D_FF + lo:D_FF + hi])
        t = _dot((_silu(a) * b).astype(BF16), w2_ref[lo:hi, :])
        acc = t if acc is None else acc + t
    return x + (0.5 * mod_ref[2:3, :]) * acc


def _ffn_final_kernel(x_ref, mod_ref, g_ref, w13_ref, w2_ref, fg_ref, o_ref):
    y = _ffn_rows(x_ref[...], mod_ref, g_ref, w13_ref, w2_ref)
    o_ref[...] = (y * lax.rsqrt(jnp.mean(y * y, axis=-1, keepdims=True) + EPS)) * fg_ref[...]


def _ffn_pair_kernel(x_ref, c_ref, modx_ref, modc_ref, g_ref, w13_ref, w2_ref, ox_ref, oc_ref):
    i = pl.program_id(0)
    last = pl.num_programs(0) - 1

    @pl.when(i < last)
    def _():
        ox_ref[...] = _ffn_rows(x_ref[...], modx_ref, g_ref, w13_ref, w2_ref)

    @pl.when(i == last)
    def _():
        oc_ref[...] = _ffn_rows(c_ref[...], modc_ref, g_ref, w13_ref, w2_ref)


def _ffn_weight_specs(w13_all, w2_all, layer, slot):
    pick = lambda i: (layer, slot, 0, 0)
    return [pl.BlockSpec((None, None) + w13_all.shape[2:], pick, pipeline_mode=pl.Buffered(1)),
            pl.BlockSpec((None, None) + w2_all.shape[2:], pick, pipeline_mode=pl.Buffered(1))]


def _ffn_final_call(x, mod3, g, w13_all, w2_all, layer, slot, fg):
    n, d = x.shape
    tm = _tile(n, FFN_ROWS)
    fix = lambda i: (0, 0)
    return pl.pallas_call(
        _ffn_final_kernel,
        grid=(n // tm,),
        in_specs=[pl.BlockSpec((tm, d), lambda i: (i, 0)), pl.BlockSpec((3, d), fix), pl.BlockSpec((1, d), fix)]
        + _ffn_weight_specs(w13_all, w2_all, layer, slot) + [pl.BlockSpec((1, d), fix)],
        out_specs=pl.BlockSpec((tm, d), lambda i: (i, 0)),
        out_shape=jax.ShapeDtypeStruct((n, d), F32),
        compiler_params=_params(("parallel",)),
        name="ffn_half",
    )(x, mod3, g, w13_all, w2_all, fg)


def _ffn_pair_call(x, c, modx3, modc3, g, w13_all, w2_all, layer, slot):
    n, d = x.shape
    nc = c.shape[0]
    tm = _tile(n, FFN_ROWS)
    nt = n // tm
    fix = lambda i: (0, 0)
    lat = lambda i: (jnp.minimum(i, nt - 1), 0)
    return pl.pallas_call(
        _ffn_pair_kernel,
        grid=(nt + 1,),
        in_specs=[pl.BlockSpec((tm, d), lat), pl.BlockSpec((nc, d), fix),
                  pl.BlockSpec((3, d), fix), pl.BlockSpec((3, d), fix), pl.BlockSpec((1, d), fix)]
        + _ffn_weight_specs(w13_all, w2_all, layer, slot),
        out_specs=[pl.BlockSpec((tm, d), lat), pl.BlockSpec((nc, d), fix)],
        out_shape=[jax.ShapeDtypeStruct((n, d), F32), jax.ShapeDtypeStruct((nc, d), F32)],
        compiler_params=_params(("arbitrary",)),
        name="ffn_half_pair",
    )(x, c, modx3, modc3, g, w13_all, w2_all)


def _head_norm_rope(t, gain, cos, sin, scale):
    ms = jnp.sum(t * t, axis=0, keepdims=True) * (1.0 / HEAD_DIM)
    t = (t * lax.rsqrt(ms + EPS)) * gain
    quarter = HEAD_DIM // 4
    parts = []
    for axis in range(2):
        x1 = t[2 * axis * quarter:(2 * axis + 1) * quarter]
        x2 = t[(2 * axis + 1) * quarter:(2 * axis + 2) * quarter]
        c = cos[axis * quarter:(axis + 1) * quarter]
        s = sin[axis * quarter:(axis + 1) * quarter]
        parts += [x1 * c - x2 * s, x1 * s + x2 * c]
    out = jnp.concatenate(parts, axis=0)
    return out * scale if scale != 1.0 else out


def _inproj_kernel(x_ref, mod_ref, g_ref, wqkv_ref, wrest_ref, qg_ref, kg_ref, cos_ref, sin_ref,
                   q_ref, k_ref, v_ref, f_ref, h_ref, s_ref, gb_ref):
    u = _modnorm(x_ref[...], g_ref[...], mod_ref[0:1, :], mod_ref[1:2, :]).astype(BF16)
    qkv = _dot_nt(wqkv_ref[...], u)
    cos = cos_ref[...]
    sin = sin_ref[...]
    zeros = jnp.zeros((HEAD_DIM, qkv.shape[1]), F32)
    for h in range(N_HEADS):
        t = _head_norm_rope(qkv[h * HEAD_DIM:(h + 1) * HEAD_DIM], qg_ref[...], cos, sin, Q_SCALE)
        blk = [t, zeros] if h // Q_GROUP == 0 else [zeros, t]
        q_ref[h * KV_W:(h + 1) * KV_W, :] = jnp.concatenate(blk, axis=0).astype(BF16)
    kt = [_head_norm_rope(qkv[Q_W + h * HEAD_DIM:Q_W + (h + 1) * HEAD_DIM], kg_ref[...], cos, sin, 1.0)
          for h in range(N_KV_HEADS)]
    k_ref[...] = jnp.concatenate(kt, axis=0).T.astype(BF16)
    ones = jnp.ones((V_ROWS - HEAD_DIM, qkv.shape[1]), F32)
    vt = []
    for h in range(N_KV_HEADS):
        vt += [qkv[Q_W + KV_W + h * HEAD_DIM:Q_W + KV_W + (h + 1) * HEAD_DIM], ones]
    v_ref[...] = jnp.concatenate(vt, axis=0).astype(BF16)

    r = _dot(u, wrest_ref[...])
    w = FNET_WIDTH
    if f_ref.shape[1] == w:
        f_ref[...] = r[:, 0:w]
    else:
        f_ref[...] = r[:, 0:w].reshape(f_ref.shape[0], -1, w).reshape(f_ref.shape)
    h_ref[...] = r[:, w:2 * w] * jax.nn.sigmoid(r[:, 2 * w:3 * w])
    gb_ref[...] = r[:, 3 * w:4 * w]
    s_ref[...] = r[:, 4 * w:5 * w] * r[:, 5 * w:6 * w]


def _inproj_call(x, mod2, g, wqkv_t, wrest, qg, kg, cos_t, sin_t):
    n, d = x.shape
    tm = _tile(n, INPROJ_ROWS)
    row = lambda i: (i, 0)
    col = lambda i: (0, i)
    fix = lambda i: (0, 0)
    w = FNET_WIDTH
    f_rows = _split_len(n)[1] if n > FOURIER_DIRECT_MAX else 1
    return pl.pallas_call(
        _inproj_kernel,
        grid=(n // tm,),
        in_specs=[
            pl.BlockSpec((tm, d), row),
            pl.BlockSpec((2, d), fix),
            pl.BlockSpec((1, d), fix),
            pl.BlockSpec((QKV_W, d), fix),
            pl.BlockSpec((d, REST_W), fix),
            pl.BlockSpec((HEAD_DIM, 1), fix),
            pl.BlockSpec((HEAD_DIM, 1), fix),
            pl.BlockSpec((HEAD_DIM // 2, tm), col),
            pl.BlockSpec((HEAD_DIM // 2, tm), col),
        ],
        out_specs=[
            pl.BlockSpec((N_HEADS * KV_W, tm), col),
            pl.BlockSpec((tm, KV_W), row),
            pl.BlockSpec((N_KV_HEADS * V_ROWS, tm), col),
            pl.BlockSpec((tm // f_rows, f_rows * w), row),
            pl.BlockSpec((tm, w), row),
            pl.BlockSpec((tm, w), row),
            pl.BlockSpec((tm, w), row),
        ],
        out_shape=[
            jax.ShapeDtypeStruct((N_HEADS * KV_W, n), BF16),
            jax.ShapeDtypeStruct((n, KV_W), BF16),
            jax.ShapeDtypeStruct((N_KV_HEADS * V_ROWS, n), BF16),
            jax.ShapeDtypeStruct((n // f_rows, f_rows * w), F32),
            jax.ShapeDtypeStruct((n, w), F32),
            jax.ShapeDtypeStruct((n, w), F32),
            jax.ShapeDtypeStruct((n, w), F32),
        ],
        compiler_params=_params(("parallel",)),
        name="in_proj",
    )(x, mod2, g, wqkv_t, wrest, qg, kg, cos_t, sin_t)


def _attn_kernel(safe_ref, q_ref, *refs, tk, n_sources):
    k_ref, v_ref = refs[0], refs[1]
    tail = refs[2:4] if n_sources == 2 else None
    o_ref, m_ref, acc_ref = refs[2 * n_sources:]
    n_chunks = k_ref.shape[0] // tk
    acc_ref[...] = jnp.zeros_like(acc_ref)

    def chunk_of(j):
        start = pl.multiple_of(j * tk, tk)
        return k_ref[pl.ds(start, tk), :], v_ref[:, pl.ds(start, tk)]

    def add_fast(kb, vb):
        for h in range(Q_GROUP):
            rows = slice(h * V_ROWS, (h + 1) * V_ROWS)
            s = _dot(kb, q_ref[h * KV_W:(h + 1) * KV_W, :])
            acc_ref[rows, :] += _dot(vb, jnp.exp2(s).astype(BF16))

    def add_slow(kb, vb):
        for h in range(Q_GROUP):
            rows = slice(h * V_ROWS, (h + 1) * V_ROWS)
            s = _dot(kb, q_ref[h * KV_W:(h + 1) * KV_W, :])
            m_old = m_ref[h:h + 1, :]
            m_new = jnp.maximum(m_old, jnp.max(s, axis=0, keepdims=True))
            acc_ref[rows, :] = (jnp.exp2(m_old - m_new) * acc_ref[rows, :]
                                + _dot(vb, jnp.exp2(s - m_new).astype(BF16)))
            m_ref[h:h + 1, :] = m_new

    def sweep(add):
        def chunk(j, carry):
            add(*chunk_of(j))
            return carry

        lax.fori_loop(0, n_chunks, chunk, 0)
        if tail is not None:
            add(tail[0][...], tail[1][...])

    @pl.when(safe_ref[0] != 0)
    def _():
        sweep(add_fast)

    @pl.when(safe_ref[0] == 0)
    def _():
        m_ref[...] = jnp.full_like(m_ref, -jnp.inf)
        sweep(add_slow)

    outs = []
    for h in range(Q_GROUP):
        r0 = h * V_ROWS
        outs.append(acc_ref[r0:r0 + HEAD_DIM, :] / acc_ref[r0 + HEAD_DIM:r0 + HEAD_DIM + 1, :])
    o_ref[...] = jnp.concatenate(outs, axis=0).T.astype(BF16)


def _attn_call(safe, q_t, sources):
    nq = q_t.shape[1]
    tq = _tile(nq, ATTN_QUERIES)
    tk = _tile(sources[0][0].shape[0], ATTN_KEYS)
    gw = Q_GROUP * HEAD_DIM
    kv_specs, kv_args = [], []
    for k, v_t in sources:
        nk = k.shape[0]
        kv_specs += [pl.BlockSpec((nk, KV_W), lambda g, i: (0, 0)), pl.BlockSpec((V_ROWS, nk), lambda g, i: (g, 0))]
        kv_args += [k, v_t]
    return pl.pallas_call(
        functools.partial(_attn_kernel, tk=tk, n_sources=len(sources)),
        grid=(N_KV_HEADS, nq // tq),
        in_specs=[pl.BlockSpec(memory_space=pltpu.SMEM), pl.BlockSpec((Q_GROUP * KV_W, tq), lambda g, i: (g, i))]
        + kv_specs,
        out_specs=pl.BlockSpec((tq, gw), lambda g, i: (i, g)),
        out_shape=jax.ShapeDtypeStruct((nq, Q_W), BF16),
        scratch_shapes=[pltpu.VMEM((8, tq), F32), pltpu.VMEM((Q_GROUP * V_ROWS, tq), F32)],
        compiler_params=_params(("parallel", "parallel")),
        name="attention",
    )(safe, q_t, *kv_args)


def _split_len(n):
    b = 1 << (int(math.log2(n)) // 2)
    return n // b, b


def _channel_dft(scale):
    c = np.arange(FNET_GROUP_DIM)
    ang = 2.0 * np.pi * np.outer(c, c) / FNET_GROUP_DIM
    eye = np.eye(FNET_GROUPS)
    return np.concatenate([np.kron(eye, np.cos(ang)), -np.kron(eye, np.sin(ang))], axis=1) * scale


def _stage1_tables(n):
    a, b = _split_len(n)
    c = np.arange(a)[None, :, None]
    pos = b * np.arange(a)[None, None, :] + np.arange(b)[:, None, None]
    ang = 2.0 * np.pi * ((c * pos) % n) / n
    co, si = np.cos(ang), np.sin(ang)
    return np.concatenate([np.concatenate([co, si], axis=2), np.concatenate([-si, co], axis=2)], axis=1)


def _stage2_table(n):
    a, b = _split_len(n)
    ang = 2.0 * np.pi * np.outer(np.arange(b), np.arange(b)) / b
    eye = np.eye(8)
    return np.concatenate([np.kron(np.cos(ang), eye), np.kron(np.sin(ang), eye)], axis=1)


def _dft_tables(n):
    ang = 2.0 * np.pi * np.outer(np.arange(n), np.arange(n)) / n
    return np.concatenate([np.cos(ang), np.sin(ang)], axis=1)


def _fft_kernel(f_ref, t1_ref, d_ref, t2_ref, o_ref, mid_ref, *, nb, n1):
    i = pl.program_id(0)
    w = FNET_WIDTH
    a = f_ref.shape[0]

    @pl.when(i < n1)
    def _():
        for bl in range(nb):
            z = _dot(f_ref[:, bl * w:(bl + 1) * w].astype(BF16), d_ref[...])
            zz = jnp.concatenate([z[:, :w], z[:, w:]], axis=0).astype(BF16)
            mid_ref[i * nb + bl] = _dot(t1_ref[bl], zz).reshape(2, a, w)

    @pl.when(i >= n1)
    def _():
        nbt = mid_ref.shape[0]
        c0 = pl.multiple_of((i - n1) * 8, 8)
        re = mid_ref[:, 0, pl.ds(c0, 8), :].reshape(nbt * 8, w)
        im = mid_ref[:, 1, pl.ds(c0, 8), :].reshape(nbt * 8, w)
        st = jnp.concatenate([re, im], axis=0).astype(BF16)
        o_ref[...] = _dot(t2_ref[...], st).reshape(nbt, 8, w)


def _fft_small_kernel(f_ref, t_ref, d_ref, o_ref):
    w = FNET_WIDTH
    z = _dot(f_ref[...].astype(BF16), d_ref[...])
    zz = jnp.concatenate([z[:, :w], z[:, w:]], axis=0).astype(BF16)
    o_ref[...] = _dot(t_ref[...], zz)


def _fourier_call(f, consts):
    w = FNET_WIDTH
    n = f.size // w
    if "t1" not in consts:
        return pl.pallas_call(
            _fft_small_kernel,
            out_shape=jax.ShapeDtypeStruct((n, w), F32),
            compiler_params=pltpu.CompilerParams(vmem_limit_bytes=V7X_VMEM_LIMIT_BYTES),
            name="fourier_small",
        )(f, consts["t"], consts["d"])
    a, b = _split_len(n)
    nb = 8
    n1, n2 = b // nb, a // 8
    out = pl.pallas_call(
        functools.partial(_fft_kernel, nb=nb, n1=n1),
        grid=(n1 + n2,),
        in_specs=[
            pl.BlockSpec((a, nb * w), lambda i: (0, jnp.minimum(i, n1 - 1))),
            pl.BlockSpec((nb, 2 * a, 2 * a), lambda i: (jnp.minimum(i, n1 - 1), 0, 0)),
            pl.BlockSpec((w, 2 * w), lambda i: (0, 0)),
            pl.BlockSpec((8 * b, 16 * b), lambda i: (0, 0), pipeline_mode=pl.Buffered(1)),
        ],
        out_specs=pl.BlockSpec((b, 8, w), lambda i: (0, jnp.maximum(i - n1, 0), 0)),
        out_shape=jax.ShapeDtypeStruct((b, a, w), F32),
        scratch_shapes=[pltpu.VMEM((b, 2, a, w), F32)],
        compiler_params=_params(("arbitrary",)),
        name="fourier",
    )(f, consts["t1"], consts["d"], consts["t2"])
    return out.reshape(n, w)


def _fourier_consts(n):
    scale = 1.0 / math.sqrt(n * FNET_GROUP_DIM)
    d = jnp.asarray(_channel_dft(scale), BF16)
    if n <= FOURIER_DIRECT_MAX:
        return {"d": d, "t": jnp.asarray(_dft_tables(n), BF16)}
    return {"d": d, "t1": jnp.asarray(_stage1_tables(n), BF16), "t2": jnp.asarray(_stage2_table(n), BF16)}


def _fill_window(win_ref, prev_ref, cur_ref, next_ref, tm):
    i = pl.program_id(0)
    last = pl.num_programs(0) - 1
    win_ref[0:HALO, :] = jnp.where(i > 0, prev_ref[tm - HALO:tm, :], 0.0)
    win_ref[HALO:HALO + tm, :] = cur_ref[...]
    win_ref[HALO + tm:HALO + tm + HALO, :] = jnp.where(i < last, next_ref[0:HALO, :], 0.0)


def _depthwise(win, w_ref, taps, rows):
    first = HALO - (taps - 1) // 2
    n = win.shape[0]
    acc = None
    for r in range(8):
        offs = [o for o in range(first, first + taps) if o % 8 == r]
        if not offs:
            continue
        base = win if r == 0 else pltpu.roll(win, n - r, 0)
        for o in offs:
            term = base[o - r:o - r + rows] * w_ref[o - first:o - first + 1, :]
            acc = term if acc is None else acc + term
    return acc


def _merge_kernel(x_ref, mod_ref, g_ref, attn_ref, fn_ref, hp_ref, hc_ref, hn_ref, sp_ref, sc_ref, sn_ref,
                  gb_ref, wg_ref, bg_ref, wa_ref, wf_ref, wc_ref, ws_ref, wo_ref,
                  dww_ref, dwb_ref, lng_ref, lnb_ref, scw_ref, o_ref, hwin_ref, swin_ref):
    tm, d = x_ref.shape
    _fill_window(hwin_ref, hp_ref, hc_ref, hn_ref, tm)
    _fill_window(swin_ref, sp_ref, sc_ref, sn_ref, tm)
    sub = tm // MERGE_SPLIT
    for part in range(MERGE_SPLIT):
        r0 = part * sub
        rows = slice(r0, r0 + sub)
        x = x_ref[rows, :]
        u = _modnorm(x, g_ref[...], mod_ref[0:1, :], mod_ref[1:2, :]).astype(BF16)

        c = _depthwise(hwin_ref[r0:r0 + sub + 2 * HALO, :], dww_ref, CONV_KERNEL, sub) + dwb_ref[...]
        mu = jnp.mean(c, axis=-1, keepdims=True)
        cc = c - mu
        var = jnp.mean(cc * cc, axis=-1, keepdims=True)
        conf = _silu((cc * lax.rsqrt(var + EPS)) * lng_ref[...] + lnb_ref[...])
        short = gb_ref[rows, :] * _depthwise(swin_ref[r0:r0 + sub + 2 * HALO, :], scw_ref, SC_KERNEL, sub)

        branches = (
            (attn_ref[rows, :], wa_ref),
            (fn_ref[rows, :].astype(BF16), wf_ref),
            (conf.astype(BF16), wc_ref),
            (short.astype(BF16), ws_ref),
        )
        merged = None
        for i, (val, w_ref) in enumerate(branches):
            cols = slice(i * d, (i + 1) * d)
            gate = jax.nn.sigmoid(_dot(u, wg_ref[:, cols]) + bg_ref[:, cols])
            term = gate * _dot(val, w_ref[...])
            merged = term if merged is None else merged + term
        o_ref[rows, :] = x + mod_ref[2:3, :] * _dot(merged.astype(BF16), wo_ref[...])


def _merge_call(x, mod3, g, attn, fn, hglu, sprod, gb, lw):
    n, d = x.shape
    tm = _tile(n, MERGE_ROWS)
    nt = n // tm
    row = lambda i: (i, 0)
    prev = lambda i: (jnp.maximum(i - 1, 0), 0)
    nxt = lambda i: (jnp.minimum(i + 1, nt - 1), 0)
    fix = lambda i: (0, 0)
    w = CONV_WIDTH

    def whole(arr):
        return pl.BlockSpec(arr.shape, fix, pipeline_mode=pl.Buffered(1))

    weights = (lw["wg"], lw["bg"], lw["wa"], lw["wf"], lw["wc"], lw["ws"], lw["wo"],
               lw["dww"], lw["dwb"], lw["lng"], lw["lnb"], lw["scw"])
    return pl.pallas_call(
        _merge_kernel,
        grid=(nt,),
        in_specs=[
            pl.BlockSpec((tm, d), row),
            pl.BlockSpec((3, d), fix),
            pl.BlockSpec((1, d), fix),
            pl.BlockSpec((tm, Q_W), row),
            pl.BlockSpec((tm, FNET_WIDTH), row),
            pl.BlockSpec((tm, w), prev), pl.BlockSpec((tm, w), row), pl.BlockSpec((tm, w), nxt),
            pl.BlockSpec((tm, w), prev), pl.BlockSpec((tm, w), row), pl.BlockSpec((tm, w), nxt),
            pl.BlockSpec((tm, w), row),
        ] + [whole(a) for a in weights],
        out_specs=pl.BlockSpec((tm, d), row),
        out_shape=jax.ShapeDtypeStruct((n, d), F32),
        scratch_shapes=[pltpu.VMEM((tm + 2 * HALO, w), F32), pltpu.VMEM((tm + 2 * HALO, w), F32)],
        compiler_params=_params(("arbitrary",)),
        name="mix_merge",
    )(x, mod3, g, attn, fn, hglu, hglu, hglu, sprod, sprod, sprod, gb, *weights)


def _rope_tables_t(n_tokens):
    n_rows = n_tokens // GRID_W
    row = jnp.broadcast_to(jnp.arange(n_rows)[:, None], (n_rows, GRID_W)).reshape(-1)
    col = jnp.broadcast_to(jnp.arange(GRID_W)[None, :], (n_rows, GRID_W)).reshape(-1)
    axis_dim = HEAD_DIM // 2
    inv_freq = ROPE_THETA ** (-jnp.arange(0, axis_dim, 2, dtype=F32) / axis_dim)
    pos = jnp.stack([row, col], axis=-1).astype(F32)
    ang = pos[:, :, None] * inv_freq
    ang_t = ang.reshape(n_tokens, axis_dim).T
    return jnp.cos(ang_t), jnp.sin(ang_t)


def kernel(x, c, ctx, c_ctx, w_ada, b_ada, norm_g, ffn_w13, ffn_w2, w_in, b_gate, q_norm_g, k_norm_g,
           w_attn_out, w_fnet, conv_dw_w, conv_dw_b, conv_ln_g, conv_ln_b, w_conv_out, sc_conv_w,
           w_sc_out, w_o, final_norm_g):
    batch, seq, d = x.shape
    assert batch == 1 and d == D_MODEL
    n_ctx = ctx.shape[1]
    depth = w_ada.shape[0]

    xs = x[0]
    cs = ctx[0]
    cos_x, sin_x = _rope_tables_t(seq)
    cos_c = jnp.ones((HEAD_DIM // 2, n_ctx), F32)
    sin_c = jnp.zeros((HEAD_DIM // 2, n_ctx), F32)
    fc_x = _fourier_consts(seq)
    fc_c = _fourier_consts(n_ctx)

    cc = jnp.zeros((8, d), F32).at[0].set(c[0]).at[1].set(c_ctx)
    mod = _mod_call(cc, w_ada, b_ada)
    mod = mod.reshape(depth, 8, N_MOD, d)

    wqkv_t = jnp.swapaxes(w_in[:, :, :QKV_W], 1, 2).astype(BF16)
    wrest = w_in[:, :, QKV_W:QKV_W + REST_W].astype(BF16)
    wg = w_in[:, :, QKV_W + REST_W:].astype(BF16)
    fg = final_norm_g.reshape(1, d)

    for l in range(depth):
        last = l == depth - 1
        mod_x, mod_c = mod[l, 0], mod[l, 1]
        lw = {
            "wg": wg[l], "bg": b_gate[l].reshape(1, -1),
            "wa": w_attn_out[l].astype(BF16), "wf": w_fnet[l].astype(BF16),
            "wc": w_conv_out[l].astype(BF16), "ws": w_sc_out[l].astype(BF16), "wo": w_o[l].astype(BF16),
            "dww": conv_dw_w[l], "dwb": conv_dw_b[l].reshape(1, -1),
            "lng": conv_ln_g[l].reshape(1, -1), "lnb": conv_ln_b[l].reshape(1, -1), "scw": sc_conv_w[l],
        }
        g0, g1, g2 = (norm_g[l, s].reshape(1, d) for s in range(3))
        qg = q_norm_g[l].reshape(HEAD_DIM, 1)
        kg = k_norm_g[l].reshape(HEAD_DIM, 1)

        xs, cs = _ffn_pair_call(xs, cs, mod_x[0:3], mod_c[0:3], g0, ffn_w13, ffn_w2, l, 0)

        q_x, k_x, v_x, f_x, h_x, s_x, gb_x = _inproj_call(
            xs, mod_x[3:5], g1, wqkv_t[l], wrest[l], qg, kg, cos_x, sin_x)
        q_c, k_c, v_c, f_c, h_c, s_c, gb_c = _inproj_call(
            cs, mod_c[3:5], g1, wqkv_t[l], wrest[l], qg, kg, cos_c, sin_c)

        bound = 8.0 * jnp.max(jnp.abs(q_norm_g[l])) * jnp.max(jnp.abs(k_norm_g[l]))
        safe = (bound <= SAFE_SCORE_BOUND).astype(jnp.int32).reshape(1)
        attn_x = _attn_call(safe, q_x, [(k_x, v_x), (k_c, v_c)])
        fn_x = _fourier_call(f_x, fc_x)
        xs = _merge_call(xs, mod_x[3:6], g1, attn_x, fn_x, h_x, s_x, gb_x, lw)

        if not last:
            attn_c = _attn_call(safe, q_c, [(k_c, v_c)])
            fn_c = _fourier_call(f_c, fc_c)
            cs = _merge_call(cs, mod_c[3:6], g1, attn_c, fn_c, h_c, s_c, gb_c, lw)

        if last:
            xs = _ffn_final_call(xs, mod_x[6:9], g2, ffn_w13, ffn_w2, l, 1, fg)
        else:
            xs, cs = _ffn_pair_call(xs, cs, mod_x[6:9], mod_c[6:9], g2, ffn_w13, ffn_w2, l, 1)

    return xs[None]
```

```python
import functools
import math

import numpy as np
import jax
import jax.numpy as jnp
from jax import lax
from jax.experimental import pallas as pl
from jax.experimental.pallas import tpu as pltpu

F32 = jnp.float32
BF16 = jnp.bfloat16

D_MODEL = 1024
GRID_W = 64
N_HEADS = 8
N_KV_HEADS = 2
HEAD_DIM = 64
Q_GROUP = N_HEADS // N_KV_HEADS
Q_W = N_HEADS * HEAD_DIM
KV_W = N_KV_HEADS * HEAD_DIM
ROPE_THETA = 10000.0
FNET_GROUPS = 4
FNET_GROUP_DIM = 64
FNET_WIDTH = FNET_GROUPS * FNET_GROUP_DIM
CONV_WIDTH = 256
CONV_KERNEL = 31
SC_WIDTH = 256
SC_KERNEL = 3
N_BRANCHES = 4
D_FF = 2816
N_MOD = 9
EPS = 1e-6
QKV_W = Q_W + 2 * KV_W
REST_W = FNET_WIDTH + 2 * CONV_WIDTH + 3 * SC_WIDTH
V_ROWS = HEAD_DIM + 16
Q_SCALE = HEAD_DIM ** -0.5 * math.log2(math.e)
SAFE_SCORE_BOUND = 40.0
HALO = 16
FF_CHUNK = 256
FOURIER_DIRECT_MAX = 512
MERGE_SPLIT = 2
V7X_VMEM_LIMIT_BYTES = 56 * 1024 * 1024
ADALN_COLS = 1152
FFN_ROWS = 512
INPROJ_ROWS = 1024
MERGE_ROWS = 512
ATTN_QUERIES = 1024
ATTN_KEYS = 8192
ATTN_KEYS_GUARDED = 2048


def _tile(n, target):
    if n <= target:
        return n
    t = (target // 128) * 128
    while t >= 128:
        if n % t == 0:
            return t
        t -= 128
    return n


def _params(sem):
    return pltpu.CompilerParams(dimension_semantics=sem, vmem_limit_bytes=V7X_VMEM_LIMIT_BYTES)


def _dot(a, b):
    return jnp.dot(a, b, preferred_element_type=F32)


def _dot_nt(a, b):
    return lax.dot_general(a, b, (((1,), (1,)), ((), ())), preferred_element_type=F32)


def _modnorm(x, g, shift, scale):
    y = x * lax.rsqrt(jnp.mean(x * x, axis=-1, keepdims=True) + EPS)
    return (y * g) * (1.0 + scale) + shift


def _silu(x):
    return x * jax.nn.sigmoid(x)


def _mod_kernel(c_ref, w_ref, b_ref, o_ref):
    a = _silu(c_ref[...])
    o_ref[0] = _dot(a.astype(BF16), w_ref[0].astype(BF16)) + b_ref[0]


def _mod_call(cc, w_ada, b_ada):
    depth, d, n = w_ada.shape
    tn = _tile(n, ADALN_COLS)
    return pl.pallas_call(
        _mod_kernel,
        grid=(depth, n // tn),
        in_specs=[
            pl.BlockSpec((8, d), lambda l, j: (0, 0)),
            pl.BlockSpec((1, d, tn), lambda l, j: (l, 0, j)),
            pl.BlockSpec((1, 1, tn), lambda l, j: (l, 0, j)),
        ],
        out_specs=pl.BlockSpec((1, 8, tn), lambda l, j: (l, 0, j)),
        out_shape=jax.ShapeDtypeStruct((depth, 8, n), F32),
        compiler_params=_params(("parallel", "parallel")),
        name="adaln_mod",
    )(cc, w_ada, b_ada.reshape(depth, 1, n))


def _ffn_rows(x, mod_ref, g_ref, w13_ref, w2_ref):
    h = _modnorm(x, g_ref[...], mod_ref[0:1, :], mod_ref[1:2, :]).astype(BF16)
    acc = None
    for c in range(D_FF // FF_CHUNK):
        lo, hi = c * FF_CHUNK, (c + 1) * FF_CHUNK
        a = _dot(h, w13_ref[:, lo:hi])
        b = _dot(h, w13_ref[:, D_FF + lo:D_FF + hi])
        t = _dot((_silu(a) * b).astype(BF16), w2_ref[lo:hi, :])
        acc = t if acc is None else acc + t
    return x + (0.5 * mod_ref[2:3, :]) * acc


def _ffn_final_kernel(x_ref, mod_ref, g_ref, w13_ref, w2_ref, fg_ref, o_ref):
    y = _ffn_rows(x_ref[...], mod_ref, g_ref, w13_ref, w2_ref)
    o_ref[...] = (y * lax.rsqrt(jnp.mean(y * y, axis=-1, keepdims=True) + EPS)) * fg_ref[...]


def _ffn_pair_kernel(x_ref, c_ref, modx_ref, modc_ref, g_ref, w13_ref, w2_ref, ox_ref, oc_ref):
    i = pl.program_id(0)
    last = pl.num_programs(0) - 1

    @pl.when(i < last)
    def _():
        ox_ref[...] = _ffn_rows(x_ref[...], modx_ref, g_ref, w13_ref, w2_ref)

    @pl.when(i == last)
    def _():
        oc_ref[...] = _ffn_rows(c_ref[...], modc_ref, g_ref, w13_ref, w2_ref)


def _ffn_weight_specs(w13_all, w2_all, layer, slot):
    pick = lambda i: (layer, slot, 0, 0)
    return [pl.BlockSpec((None, None) + w13_all.shape[2:], pick, pipeline_mode=pl.Buffered(1)),
            pl.BlockSpec((None, None) + w2_all.shape[2:], pick, pipeline_mode=pl.Buffered(1))]


def _ffn_final_call(x, mod3, g, w13_all, w2_all, layer, slot, fg):
    n, d = x.shape
    tm = _tile(n, FFN_ROWS)
    fix = lambda i: (0, 0)
    return pl.pallas_call(
        _ffn_final_kernel,
        grid=(n // tm,),
        in_specs=[pl.BlockSpec((tm, d), lambda i: (i, 0)), pl.BlockSpec((3, d), fix), pl.BlockSpec((1, d), fix)]
        + _ffn_weight_specs(w13_all, w2_all, layer, slot) + [pl.BlockSpec((1, d), fix)],
        out_specs=pl.BlockSpec((tm, d), lambda i: (i, 0)),
        out_shape=jax.ShapeDtypeStruct((n, d), F32),
        compiler_params=_params(("parallel",)),
        name="ffn_half",
    )(x, mod3, g, w13_all, w2_all, fg)


def _ffn_pair_call(x, c, modx3, modc3, g, w13_all, w2_all, layer, slot):
    n, d = x.shape
    nc = c.shape[0]
    tm = _tile(n, FFN_ROWS)
    nt = n // tm
    fix = lambda i: (0, 0)
    lat = lambda i: (jnp.minimum(i, nt - 1), 0)
    return pl.pallas_call(
        _ffn_pair_kernel,
        grid=(nt + 1,),
        in_specs=[pl.BlockSpec((tm, d), lat), pl.BlockSpec((nc, d), fix),
                  pl.BlockSpec((3, d), fix), pl.BlockSpec((3, d), fix), pl.BlockSpec((1, d), fix)]
        + _ffn_weight_specs(w13_all, w2_all, layer, slot),
        out_specs=[pl.BlockSpec((tm, d), lat), pl.BlockSpec((nc, d), fix)],
        out_shape=[jax.ShapeDtypeStruct((n, d), F32), jax.ShapeDtypeStruct((nc, d), F32)],
        compiler_params=_params(("arbitrary",)),
        name="ffn_half_pair",
    )(x, c, modx3, modc3, g, w13_all, w2_all)


def _head_norm_rope(t, gain, cos, sin, scale):
    ms = jnp.sum(t * t, axis=0, keepdims=True) * (1.0 / HEAD_DIM)
    t = (t * lax.rsqrt(ms + EPS)) * gain
    quarter = HEAD_DIM // 4
    parts = []
    for axis in range(2):
        x1 = t[2 * axis * quarter:(2 * axis + 1) * quarter]
        x2 = t[(2 * axis + 1) * quarter:(2 * axis + 2) * quarter]
        c = cos[axis * quarter:(axis + 1) * quarter]
        s = sin[axis * quarter:(axis + 1) * quarter]
        parts += [x1 * c - x2 * s, x1 * s + x2 * c]
    out = jnp.concatenate(parts, axis=0)
    return out * scale if scale != 1.0 else out


def _inproj_kernel(x_ref, mod_ref, g_ref, wqkv_ref, wrest_ref, qg_ref, kg_ref, cos_ref, sin_ref,
                   q_ref, k_ref, v_ref, f_ref, h_ref, s_ref, gb_ref):
    u = _modnorm(x_ref[...], g_ref[...], mod_ref[0:1, :], mod_ref[1:2, :]).astype(BF16)
    qkv = _dot_nt(wqkv_ref[...], u)
    cos = cos_ref[...]
    sin = sin_ref[...]
    zeros = jnp.zeros((HEAD_DIM, qkv.shape[1]), F32)
    for h in range(N_HEADS):
        t = _head_norm_rope(qkv[h * HEAD_DIM:(h + 1) * HEAD_DIM], qg_ref[...], cos, sin, Q_SCALE)
        blk = [t, zeros] if h // Q_GROUP == 0 else [zeros, t]
        q_ref[h * KV_W:(h + 1) * KV_W, :] = jnp.concatenate(blk, axis=0).astype(BF16)
    kt = [_head_norm_rope(qkv[Q_W + h * HEAD_DIM:Q_W + (h + 1) * HEAD_DIM], kg_ref[...], cos, sin, 1.0)
          for h in range(N_KV_HEADS)]
    k_ref[...] = jnp.concatenate(kt, axis=0).T.astype(BF16)
    ones = jnp.ones((V_ROWS - HEAD_DIM, qkv.shape[1]), F32)
    vt = []
    for h in range(N_KV_HEADS):
        vt += [qkv[Q_W + KV_W + h * HEAD_DIM:Q_W + KV_W + (h + 1) * HEAD_DIM], ones]
    v_ref[...] = jnp.concatenate(vt, axis=0).astype(BF16)

    r = _dot(u, wrest_ref[...])
    w = FNET_WIDTH
    if f_ref.shape[1] == w:
        f_ref[...] = r[:, 0:w]
    else:
        f_ref[...] = r[:, 0:w].reshape(f_ref.shape[0], -1, w).reshape(f_ref.shape)
    h_ref[...] = r[:, w:2 * w] * jax.nn.sigmoid(r[:, 2 * w:3 * w])
    gb_ref[...] = r[:, 3 * w:4 * w]
    s_ref[...] = r[:, 4 * w:5 * w] * r[:, 5 * w:6 * w]


def _inproj_call(x, mod2, g, wqkv_t, wrest, qg, kg, cos_t, sin_t):
    n, d = x.shape
    tm = _tile(n, INPROJ_ROWS)
    row = lambda i: (i, 0)
    col = lambda i: (0, i)
    fix = lambda i: (0, 0)
    w = FNET_WIDTH
    f_rows = _split_len(n)[1] if n > FOURIER_DIRECT_MAX else 1
    return pl.pallas_call(
        _inproj_kernel,
        grid=(n // tm,),
        in_specs=[
            pl.BlockSpec((tm, d), row),
            pl.BlockSpec((2, d), fix),
            pl.BlockSpec((1, d), fix),
            pl.BlockSpec((QKV_W, d), fix),
            pl.BlockSpec((d, REST_W), fix),
            pl.BlockSpec((HEAD_DIM, 1), fix),
            pl.BlockSpec((HEAD_DIM, 1), fix),
            pl.BlockSpec((HEAD_DIM // 2, tm), col),
            pl.BlockSpec((HEAD_DIM // 2, tm), col),
        ],
        out_specs=[
            pl.BlockSpec((N_HEADS * KV_W, tm), col),
            pl.BlockSpec((tm, KV_W), row),
            pl.BlockSpec((N_KV_HEADS * V_ROWS, tm), col),
            pl.BlockSpec((tm // f_rows, f_rows * w), row),
            pl.BlockSpec((tm, w), row),
            pl.BlockSpec((tm, w), row),
            pl.BlockSpec((tm, w), row),
        ],
        out_shape=[
            jax.ShapeDtypeStruct((N_HEADS * KV_W, n), BF16),
            jax.ShapeDtypeStruct((n, KV_W), BF16),
            jax.ShapeDtypeStruct((N_KV_HEADS * V_ROWS, n), BF16),
            jax.ShapeDtypeStruct((n // f_rows, f_rows * w), F32),
            jax.ShapeDtypeStruct((n, w), F32),
            jax.ShapeDtypeStruct((n, w), F32),
            jax.ShapeDtypeStruct((n, w), F32),
        ],
        compiler_params=_params(("parallel",)),
        name="in_proj",
    )(x, mod2, g, wqkv_t, wrest, qg, kg, cos_t, sin_t)


def _attn_kernel(safe_ref, q_ref, *refs, tk, n_sources):
    k_ref, v_ref = refs[0], refs[1]
    tail = refs[2:4] if n_sources == 2 else None
    o_ref, m_ref, acc_ref = refs[2 * n_sources:]
    n_keys = k_ref.shape[0]
    acc_ref[...] = jnp.zeros_like(acc_ref)

    def add_fast(kb, vb):
        for h in range(Q_GROUP):
            rows = slice(h * V_ROWS, (h + 1) * V_ROWS)
            s = _dot(kb, q_ref[h * KV_W:(h + 1) * KV_W, :])
            acc_ref[rows, :] += _dot(vb, jnp.exp2(s).astype(BF16))

    def add_slow(kb, vb):
        for h in range(Q_GROUP):
            rows = slice(h * V_ROWS, (h + 1) * V_ROWS)
            s = _dot(kb, q_ref[h * KV_W:(h + 1) * KV_W, :])
            m_old = m_ref[h:h + 1, :]
            m_new = jnp.maximum(m_old, jnp.max(s, axis=0, keepdims=True))
            acc_ref[rows, :] = (jnp.exp2(m_old - m_new) * acc_ref[rows, :]
                                + _dot(vb, jnp.exp2(s - m_new).astype(BF16)))
            m_ref[h:h + 1, :] = m_new

    def sweep(add, chunk_keys):
        def chunk(j, carry):
            start = pl.multiple_of(j * chunk_keys, chunk_keys)
            add(k_ref[pl.ds(start, chunk_keys), :], v_ref[:, pl.ds(start, chunk_keys)])
            return carry

        lax.fori_loop(0, n_keys // chunk_keys, chunk, 0)
        if tail is not None:
            add(tail[0][...], tail[1][...])

    @pl.when(safe_ref[0] != 0)
    def _():
        sweep(add_fast, tk)

    @pl.when(safe_ref[0] == 0)
    def _():
        m_ref[...] = jnp.full_like(m_ref, -jnp.inf)
        sweep(add_slow, min(tk, ATTN_KEYS_GUARDED))

    outs = []
    for h in range(Q_GROUP):
        r0 = h * V_ROWS
        outs.append(acc_ref[r0:r0 + HEAD_DIM, :] / acc_ref[r0 + HEAD_DIM:r0 + HEAD_DIM + 1, :])
    o_ref[...] = jnp.concatenate(outs, axis=0).T.astype(BF16)


def _attn_call(safe, q_t, sources):
    nq = q_t.shape[1]
    tq = _tile(nq, ATTN_QUERIES)
    tk = _tile(sources[0][0].shape[0], ATTN_KEYS)
    gw = Q_GROUP * HEAD_DIM
    kv_specs, kv_args = [], []
    for k, v_t in sources:
        nk = k.shape[0]
        kv_specs += [pl.BlockSpec((nk, KV_W), lambda g, i: (0, 0)), pl.BlockSpec((V_ROWS, nk), lambda g, i: (g, 0))]
        kv_args += [k, v_t]
    return pl.pallas_call(
        functools.partial(_attn_kernel, tk=tk, n_sources=len(sources)),
        grid=(N_KV_HEADS, nq // tq),
        in_specs=[pl.BlockSpec(memory_space=pltpu.SMEM), pl.BlockSpec((Q_GROUP * KV_W, tq), lambda g, i: (g, i))]
        + kv_specs,
        out_specs=pl.BlockSpec((tq, gw), lambda g, i: (i, g)),
        out_shape=jax.ShapeDtypeStruct((nq, Q_W), BF16),
        scratch_shapes=[pltpu.VMEM((8, tq), F32), pltpu.VMEM((Q_GROUP * V_ROWS, tq), F32)],
        compiler_params=_params(("parallel", "parallel")),
        name="attention",
    )(safe, q_t, *kv_args)


def _split_len(n):
    b = 1 << (int(math.log2(n)) // 2)
    return n // b, b


def _channel_dft(scale):
    c = np.arange(FNET_GROUP_DIM)
    ang = 2.0 * np.pi * np.outer(c, c) / FNET_GROUP_DIM
    eye = np.eye(FNET_GROUPS)
    return np.concatenate([np.kron(eye, np.cos(ang)), -np.kron(eye, np.sin(ang))], axis=1) * scale


def _stage1_tables(n):
    a, b = _split_len(n)
    c = np.arange(a)[None, :, None]
    pos = b * np.arange(a)[None, None, :] + np.arange(b)[:, None, None]
    ang = 2.0 * np.pi * ((c * pos) % n) / n
    co, si = np.cos(ang), np.sin(ang)
    return np.concatenate([np.concatenate([co, si], axis=2), np.concatenate([-si, co], axis=2)], axis=1)


def _stage2_table(n):
    a, b = _split_len(n)
    ang = 2.0 * np.pi * np.outer(np.arange(b), np.arange(b)) / b
    eye = np.eye(8)
    return np.concatenate([np.kron(np.cos(ang), eye), np.kron(np.sin(ang), eye)], axis=1)


def _dft_tables(n):
    ang = 2.0 * np.pi * np.outer(np.arange(n), np.arange(n)) / n
    return np.concatenate([np.cos(ang), np.sin(ang)], axis=1)


def _fft_kernel(f_ref, t1_ref, d_ref, t2_ref, o_ref, mid_ref, *, nb, n1):
    i = pl.program_id(0)
    w = FNET_WIDTH
    a = f_ref.shape[0]

    @pl.when(i < n1)
    def _():
        for bl in range(nb):
            z = _dot(f_ref[:, bl * w:(bl + 1) * w].astype(BF16), d_ref[...])
            zz = jnp.concatenate([z[:, :w], z[:, w:]], axis=0).astype(BF16)
            mid_ref[i * nb + bl] = _dot(t1_ref[bl], zz).reshape(2, a, w)

    @pl.when(i >= n1)
    def _():
        nbt = mid_ref.shape[0]
        c0 = pl.multiple_of((i - n1) * 8, 8)
        re = mid_ref[:, 0, pl.ds(c0, 8), :].reshape(nbt * 8, w)
        im = mid_ref[:, 1, pl.ds(c0, 8), :].reshape(nbt * 8, w)
        st = jnp.concatenate([re, im], axis=0).astype(BF16)
        o_ref[...] = _dot(t2_ref[...], st).reshape(nbt, 8, w)


def _fft_small_kernel(f_ref, t_ref, d_ref, o_ref):
    w = FNET_WIDTH
    z = _dot(f_ref[...].astype(BF16), d_ref[...])
    zz = jnp.concatenate([z[:, :w], z[:, w:]], axis=0).astype(BF16)
    o_ref[...] = _dot(t_ref[...], zz)


def _fourier_call(f, consts):
    w = FNET_WIDTH
    n = f.size // w
    if "t1" not in consts:
        return pl.pallas_call(
            _fft_small_kernel,
            out_shape=jax.ShapeDtypeStruct((n, w), F32),
            compiler_params=pltpu.CompilerParams(vmem_limit_bytes=V7X_VMEM_LIMIT_BYTES),
            name="fourier_small",
        )(f, consts["t"], consts["d"])
    a, b = _split_len(n)
    nb = 8
    n1, n2 = b // nb, a // 8
    out = pl.pallas_call(
        functools.partial(_fft_kernel, nb=nb, n1=n1),
        grid=(n1 + n2,),
        in_specs=[
            pl.BlockSpec((a, nb * w), lambda i: (0, jnp.minimum(i, n1 - 1))),
            pl.BlockSpec((nb, 2 * a, 2 * a), lambda i: (jnp.minimum(i, n1 - 1), 0, 0)),
            pl.BlockSpec((w, 2 * w), lambda i: (0, 0)),
            pl.BlockSpec((8 * b, 16 * b), lambda i: (0, 0), pipeline_mode=pl.Buffered(1)),
        ],
        out_specs=pl.BlockSpec((b, 8, w), lambda i: (0, jnp.maximum(i - n1, 0), 0)),
        out_shape=jax.ShapeDtypeStruct((b, a, w), F32),
        scratch_shapes=[pltpu.VMEM((b, 2, a, w), F32)],
        compiler_params=_params(("arbitrary",)),
        name="fourier",
    )(f, consts["t1"], consts["d"], consts["t2"])
    return out.reshape(n, w)


def _fourier_consts(n):
    scale = 1.0 / math.sqrt(n * FNET_GROUP_DIM)
    d = jnp.asarray(_channel_dft(scale), BF16)
    if n <= FOURIER_DIRECT_MAX:
        return {"d": d, "t": jnp.asarray(_dft_tables(n), BF16)}
    return {"d": d, "t1": jnp.asarray(_stage1_tables(n), BF16), "t2": jnp.asarray(_stage2_table(n), BF16)}


def _fill_window(win_ref, prev_ref, cur_ref, next_ref, tm):
    i = pl.program_id(0)
    last = pl.num_programs(0) - 1
    win_ref[0:HALO, :] = jnp.where(i > 0, prev_ref[tm - HALO:tm, :], 0.0)
    win_ref[HALO:HALO + tm, :] = cur_ref[...]
    win_ref[HALO + tm:HALO + tm + HALO, :] = jnp.where(i < last, next_ref[0:HALO, :], 0.0)


def _depthwise(win, w_ref, taps, rows):
    first = HALO - (taps - 1) // 2
    n = win.shape[0]
    acc = None
    for r in range(8):
        offs = [o for o in range(first, first + taps) if o % 8 == r]
        if not offs:
            continue
        base = win if r == 0 else pltpu.roll(win, n - r, 0)
        for o in offs:
            term = base[o - r:o - r + rows] * w_ref[o - first:o - first + 1, :]
            acc = term if acc is None else acc + term
    return acc


def _merge_kernel(x_ref, mod_ref, g_ref, attn_ref, fn_ref, hp_ref, hc_ref, hn_ref, sp_ref, sc_ref, sn_ref,
                  gb_ref, wg_ref, bg_ref, wa_ref, wf_ref, wc_ref, ws_ref, wo_ref,
                  dww_ref, dwb_ref, lng_ref, lnb_ref, scw_ref, o_ref, hwin_ref, swin_ref):
    tm, d = x_ref.shape
    _fill_window(hwin_ref, hp_ref, hc_ref, hn_ref, tm)
    _fill_window(swin_ref, sp_ref, sc_ref, sn_ref, tm)
    sub = tm // MERGE_SPLIT
    for part in range(MERGE_SPLIT):
        r0 = part * sub
        rows = slice(r0, r0 + sub)
        x = x_ref[rows, :]
        u = _modnorm(x, g_ref[...], mod_ref[0:1, :], mod_ref[1:2, :]).astype(BF16)

        c = _depthwise(hwin_ref[r0:r0 + sub + 2 * HALO, :], dww_ref, CONV_KERNEL, sub) + dwb_ref[...]
        mu = jnp.mean(c, axis=-1, keepdims=True)
        cc = c - mu
        var = jnp.mean(cc * cc, axis=-1, keepdims=True)
        conf = _silu((cc * lax.rsqrt(var + EPS)) * lng_ref[...] + lnb_ref[...])
        short = gb_ref[rows, :] * _depthwise(swin_ref[r0:r0 + sub + 2 * HALO, :], scw_ref, SC_KERNEL, sub)

        branches = (
            (attn_ref[rows, :], wa_ref),
            (fn_ref[rows, :].astype(BF16), wf_ref),
            (conf.astype(BF16), wc_ref),
            (short.astype(BF16), ws_ref),
        )
        merged = None
        for i, (val, w_ref) in enumerate(branches):
            cols = slice(i * d, (i + 1) * d)
            gate = jax.nn.sigmoid(_dot(u, wg_ref[:, cols]) + bg_ref[:, cols])
            term = gate * _dot(val, w_ref[...])
            merged = term if merged is None else merged + term
        o_ref[rows, :] = x + mod_ref[2:3, :] * _dot(merged.astype(BF16), wo_ref[...])


def _merge_call(x, mod3, g, attn, fn, hglu, sprod, gb, lw):
    n, d = x.shape
    tm = _tile(n, MERGE_ROWS)
    nt = n // tm
    row = lambda i: (i, 0)
    prev = lambda i: (jnp.maximum(i - 1, 0), 0)
    nxt = lambda i: (jnp.minimum(i + 1, nt - 1), 0)
    fix = lambda i: (0, 0)
    w = CONV_WIDTH

    def whole(arr):
        return pl.BlockSpec(arr.shape, fix, pipeline_mode=pl.Buffered(1))

    weights = (lw["wg"], lw["bg"], lw["wa"], lw["wf"], lw["wc"], lw["ws"], lw["wo"],
               lw["dww"], lw["dwb"], lw["lng"], lw["lnb"], lw["scw"])
    return pl.pallas_call(
        _merge_kernel,
        grid=(nt,),
        in_specs=[
            pl.BlockSpec((tm, d), row),
            pl.BlockSpec((3, d), fix),
            pl.BlockSpec((1, d), fix),
            pl.BlockSpec((tm, Q_W), row),
            pl.BlockSpec((tm, FNET_WIDTH), row),
            pl.BlockSpec((tm, w), prev), pl.BlockSpec((tm, w), row), pl.BlockSpec((tm, w), nxt),
            pl.BlockSpec((tm, w), prev), pl.BlockSpec((tm, w), row), pl.BlockSpec((tm, w), nxt),
            pl.BlockSpec((tm, w), row),
        ] + [whole(a) for a in weights],
        out_specs=pl.BlockSpec((tm, d), row),
        out_shape=jax.ShapeDtypeStruct((n, d), F32),
        scratch_shapes=[pltpu.VMEM((tm + 2 * HALO, w), F32), pltpu.VMEM((tm + 2 * HALO, w), F32)],
        compiler_params=_params(("arbitrary",)),
        name="mix_merge",
    )(x, mod3, g, attn, fn, hglu, hglu, hglu, sprod, sprod, sprod, gb, *weights)


def _rope_tables_t(n_tokens):
    n_rows = n_tokens // GRID_W
    row = jnp.broadcast_to(jnp.arange(n_rows)[:, None], (n_rows, GRID_W)).reshape(-1)
    col = jnp.broadcast_to(jnp.arange(GRID_W)[None, :], (n_rows, GRID_W)).reshape(-1)
    axis_dim = HEAD_DIM // 2
    inv_freq = ROPE_THETA ** (-jnp.arange(0, axis_dim, 2, dtype=F32) / axis_dim)
    pos = jnp.stack([row, col], axis=-1).astype(F32)
    ang = pos[:, :, None] * inv_freq
    ang_t = ang.reshape(n_tokens, axis_dim).T
    return jnp.cos(ang_t), jnp.sin(ang_t)


def kernel(x, c, ctx, c_ctx, w_ada, b_ada, norm_g, ffn_w13, ffn_w2, w_in, b_gate, q_norm_g, k_norm_g,
           w_attn_out, w_fnet, conv_dw_w, conv_dw_b, conv_ln_g, conv_ln_b, w_conv_out, sc_conv_w,
           w_sc_out, w_o, final_norm_g):
    batch, seq, d = x.shape
    assert batch == 1 and d == D_MODEL
    n_ctx = ctx.shape[1]
    depth = w_ada.shape[0]

    xs = x[0]
    cs = ctx[0]
    cos_x, sin_x = _rope_tables_t(seq)
    cos_c = jnp.ones((HEAD_DIM // 2, n_ctx), F32)
    sin_c = jnp.zeros((HEAD_DIM // 2, n_ctx), F32)
    fc_x = _fourier_consts(seq)
    fc_c = _fourier_consts(n_ctx)

    cc = jnp.zeros((8, d), F32).at[0].set(c[0]).at[1].set(c_ctx)
    mod = _mod_call(cc, w_ada, b_ada)
    mod = mod.reshape(depth, 8, N_MOD, d)

    wqkv_t = jnp.swapaxes(w_in[:, :, :QKV_W], 1, 2).astype(BF16)
    wrest = w_in[:, :, QKV_W:QKV_W + REST_W].astype(BF16)
    wg = w_in[:, :, QKV_W + REST_W:].astype(BF16)
    fg = final_norm_g.reshape(1, d)

    for l in range(depth):
        last = l == depth - 1
        mod_x, mod_c = mod[l, 0], mod[l, 1]
        lw = {
            "wg": wg[l], "bg": b_gate[l].reshape(1, -1),
            "wa": w_attn_out[l].astype(BF16), "wf": w_fnet[l].astype(BF16),
            "wc": w_conv_out[l].astype(BF16), "ws": w_sc_out[l].astype(BF16), "wo": w_o[l].astype(BF16),
            "dww": conv_dw_w[l], "dwb": conv_dw_b[l].reshape(1, -1),
            "lng": conv_ln_g[l].reshape(1, -1), "lnb": conv_ln_b[l].reshape(1, -1), "scw": sc_conv_w[l],
        }
        g0, g1, g2 = (norm_g[l, s].reshape(1, d) for s in range(3))
        qg = q_norm_g[l].reshape(HEAD_DIM, 1)
        kg = k_norm_g[l].reshape(HEAD_DIM, 1)

        xs, cs = _ffn_pair_call(xs, cs, mod_x[0:3], mod_c[0:3], g0, ffn_w13, ffn_w2, l, 0)

        q_x, k_x, v_x, f_x, h_x, s_x, gb_x = _inproj_call(
            xs, mod_x[3:5], g1, wqkv_t[l], wrest[l], qg, kg, cos_x, sin_x)
        q_c, k_c, v_c, f_c, h_c, s_c, gb_c = _inproj_call(
            cs, mod_c[3:5], g1, wqkv_t[l], wrest[l], qg, kg, cos_c, sin_c)

        bound = 8.0 * jnp.max(jnp.abs(q_norm_g[l])) * jnp.max(jnp.abs(k_norm_g[l]))
        safe = (bound <= SAFE_SCORE_BOUND).astype(jnp.int32).reshape(1)
        attn_x = _attn_call(safe, q_x, [(k_x, v_x), (k_c, v_c)])
        fn_x = _fourier_call(f_x, fc_x)
        xs = _merge_call(xs, mod_x[3:6], g1, attn_x, fn_x, h_x, s_x, gb_x, lw)

        if not last:
            attn_c = _attn_call(safe, q_c, [(k_c, v_c)])
            fn_c = _fourier_call(f_c, fc_c)
            cs = _merge_call(cs, mod_c[3:6], g1, attn_c, fn_c, h_c, s_c, gb_c, lw)

        if last:
            xs = _ffn_final_call(xs, mod_x[6:9], g2, ffn_w13, ffn_w2, l, 1, fg)
        else:
            xs, cs = _ffn_pair_call(xs, cs, mod_x[6:9], mod_c[6:9], g2, ffn_w13, ffn_w2, l, 1)

    return xs[None]
```

```python
import functools
import math

import numpy as np
import jax
import jax.numpy as jnp
from jax import lax
from jax.experimental import pallas as pl
from jax.experimental.pallas import tpu as pltpu

F32 = jnp.float32
BF16 = jnp.bfloat16

D_MODEL = 1024
GRID_W = 64
N_HEADS = 8
N_KV_HEADS = 2
HEAD_DIM = 64
Q_GROUP = N_HEADS // N_KV_HEADS
Q_W = N_HEADS * HEAD_DIM
KV_W = N_KV_HEADS * HEAD_DIM
ROPE_THETA = 10000.0
FNET_GROUPS = 4
FNET_GROUP_DIM = 64
FNET_WIDTH = FNET_GROUPS * FNET_GROUP_DIM
CONV_WIDTH = 256
CONV_KERNEL = 31
SC_WIDTH = 256
SC_KERNEL = 3
N_BRANCHES = 4
D_FF = 2816
N_MOD = 9
EPS = 1e-6
QKV_W = Q_W + 2 * KV_W
REST_W = FNET_WIDTH + 2 * CONV_WIDTH + 3 * SC_WIDTH
V_ROWS = HEAD_DIM + 16
Q_SCALE = HEAD_DIM ** -0.5 * math.log2(math.e)
SAFE_SCORE_BOUND = 40.0
HALO = 16
FF_CHUNK = 256
FOURIER_DIRECT_MAX = 512
MERGE_SPLIT = 2
V7X_VMEM_LIMIT_BYTES = 56 * 1024 * 1024
ADALN_COLS = 1152
FFN_ROWS = 512
INPROJ_ROWS = 1024
MERGE_ROWS = 512
ATTN_QUERIES = 1024
ATTN_KEYS = 8192


def _tile(n, target):
    if n <= target:
        return n
    t = (target // 128) * 128
    while t >= 128:
        if n % t == 0:
            return t
        t -= 128
    return n


def _params(sem):
    return pltpu.CompilerParams(dimension_semantics=sem, vmem_limit_bytes=V7X_VMEM_LIMIT_BYTES)


def _dot(a, b):
    return jnp.dot(a, b, preferred_element_type=F32)


def _dot_nt(a, b):
    return lax.dot_general(a, b, (((1,), (1,)), ((), ())), preferred_element_type=F32)


def _modnorm(x, g, shift, scale):
    y = x * lax.rsqrt(jnp.mean(x * x, axis=-1, keepdims=True) + EPS)
    return (y * g) * (1.0 + scale) + shift


def _silu(x):
    return x * jax.nn.sigmoid(x)


def _mod_kernel(c_ref, w_ref, b_ref, o_ref):
    a = _silu(c_ref[...])
    o_ref[0] = _dot(a.astype(BF16), w_ref[0].astype(BF16)) + b_ref[0]


def _mod_call(cc, w_ada, b_ada):
    depth, d, n = w_ada.shape
    tn = _tile(n, ADALN_COLS)
    return pl.pallas_call(
        _mod_kernel,
        grid=(depth, n // tn),
        in_specs=[
            pl.BlockSpec((8, d), lambda l, j: (0, 0)),
            pl.BlockSpec((1, d, tn), lambda l, j: (l, 0, j)),
            pl.BlockSpec((1, 1, tn), lambda l, j: (l, 0, j)),
        ],
        out_specs=pl.BlockSpec((1, 8, tn), lambda l, j: (l, 0, j)),
        out_shape=jax.ShapeDtypeStruct((depth, 8, n), F32),
        compiler_params=_params(("parallel", "parallel")),
        name="adaln_mod",
    )(cc, w_ada, b_ada.reshape(depth, 1, n))


def _ffn_rows(x, mod_ref, g_ref, w13_ref, w2_ref):
    h = _modnorm(x, g_ref[...], mod_ref[0:1, :], mod_ref[1:2, :]).astype(BF16)
    acc = None
    for c in range(D_FF // FF_CHUNK):
        lo, hi = c * FF_CHUNK, (c + 1) * FF_CHUNK
        a = _dot(h, w13_ref[:, lo:hi])
        b = _dot(h, w13_ref[:, D_FF + lo:D_FF + hi])
        t = _dot((_silu(a) * b).astype(BF16), w2_ref[lo:hi, :])
        acc = t if acc is None else acc + t
    return x + (0.5 * mod_ref[2:3, :]) * acc


def _ffn_final_kernel(x_ref, mod_ref, g_ref, w13_ref, w2_ref, fg_ref, o_ref):
    y = _ffn_rows(x_ref[...], mod_ref, g_ref, w13_ref, w2_ref)
    o_ref[...] = (y * lax.rsqrt(jnp.mean(y * y, axis=-1, keepdims=True) + EPS)) * fg_ref[...]


def _ffn_pair_kernel(x_ref, c_ref, modx_ref, modc_ref, g_ref, w13_ref, w2_ref, ox_ref, oc_ref):
    i = pl.program_id(0)
    last = pl.num_programs(0) - 1

    @pl.when(i < last)
    def _():
        ox_ref[...] = _ffn_rows(x_ref[...], modx_ref, g_ref, w13_ref, w2_ref)

    @pl.when(i == last)
    def _():
        oc_ref[...] = _ffn_rows(c_ref[...], modc_ref, g_ref, w13_ref, w2_ref)


def _ffn_weight_specs(w13_all, w2_all, layer, slot):
    pick = lambda i: (layer, slot, 0, 0)
    return [pl.BlockSpec((None, None) + w13_all.shape[2:], pick, pipeline_mode=pl.Buffered(1)),
            pl.BlockSpec((None, None) + w2_all.shape[2:], pick, pipeline_mode=pl.Buffered(1))]


def _ffn_final_call(x, mod3, g, w13_all, w2_all, layer, slot, fg):
    n, d = x.shape
    tm = _tile(n, FFN_ROWS)
    fix = lambda i: (0, 0)
    return pl.pallas_call(
        _ffn_final_kernel,
        grid=(n // tm,),
        in_specs=[pl.BlockSpec((tm, d), lambda i: (i, 0)), pl.BlockSpec((3, d), fix), pl.BlockSpec((1, d), fix)]
        + _ffn_weight_specs(w13_all, w2_all, layer, slot) + [pl.BlockSpec((1, d), fix)],
        out_specs=pl.BlockSpec((tm, d), lambda i: (i, 0)),
        out_shape=jax.ShapeDtypeStruct((n, d), F32),
        compiler_params=_params(("parallel",)),
        name="ffn_half",
    )(x, mod3, g, w13_all, w2_all, fg)


def _ffn_pair_call(x, c, modx3, modc3, g, w13_all, w2_all, layer, slot):
    n, d = x.shape
    nc = c.shape[0]
    tm = _tile(n, FFN_ROWS)
    nt = n // tm
    fix = lambda i: (0, 0)
    lat = lambda i: (jnp.minimum(i, nt - 1), 0)
    return pl.pallas_call(
        _ffn_pair_kernel,
        grid=(nt + 1,),
        in_specs=[pl.BlockSpec((tm, d), lat), pl.BlockSpec((nc, d), fix),
                  pl.BlockSpec((3, d), fix), pl.BlockSpec((3, d), fix), pl.BlockSpec((1, d), fix)]
        + _ffn_weight_specs(w13_all, w2_all, layer, slot),
        out_specs=[pl.BlockSpec((tm, d), lat), pl.BlockSpec((nc, d), fix)],
        out_shape=[jax.ShapeDtypeStruct((n, d), F32), jax.ShapeDtypeStruct((nc, d), F32)],
        compiler_params=_params(("arbitrary",)),
        name="ffn_half_pair",
    )(x, c, modx3, modc3, g, w13_all, w2_all)


def _head_norm_rope(t, gain, cos, sin, scale):
    ms = jnp.sum(t * t, axis=0, keepdims=True) * (1.0 / HEAD_DIM)
    t = (t * lax.rsqrt(ms + EPS)) * gain
    quarter = HEAD_DIM // 4
    parts = []
    for axis in range(2):
        x1 = t[2 * axis * quarter:(2 * axis + 1) * quarter]
        x2 = t[(2 * axis + 1) * quarter:(2 * axis + 2) * quarter]
        c = cos[axis * quarter:(axis + 1) * quarter]
        s = sin[axis * quarter:(axis + 1) * quarter]
        parts += [x1 * c - x2 * s, x1 * s + x2 * c]
    out = jnp.concatenate(parts, axis=0)
    return out * scale if scale != 1.0 else out


def _inproj_kernel(x_ref, mod_ref, g_ref, wqkv_ref, wrest_ref, qg_ref, kg_ref, cos_ref, sin_ref,
                   q_ref, k_ref, v_ref, f_ref, h_ref, s_ref, gb_ref):
    u = _modnorm(x_ref[...], g_ref[...], mod_ref[0:1, :], mod_ref[1:2, :]).astype(BF16)
    qkv = _dot_nt(wqkv_ref[...], u)
    cos = cos_ref[...]
    sin = sin_ref[...]
    zeros = jnp.zeros((HEAD_DIM, qkv.shape[1]), F32)
    for h in range(N_HEADS):
        t = _head_norm_rope(qkv[h * HEAD_DIM:(h + 1) * HEAD_DIM], qg_ref[...], cos, sin, Q_SCALE)
        blk = [t, zeros] if h // Q_GROUP == 0 else [zeros, t]
        q_ref[h * KV_W:(h + 1) * KV_W, :] = jnp.concatenate(blk, axis=0).astype(BF16)
    kt = [_head_norm_rope(qkv[Q_W + h * HEAD_DIM:Q_W + (h + 1) * HEAD_DIM], kg_ref[...], cos, sin, 1.0)
          for h in range(N_KV_HEADS)]
    k_ref[...] = jnp.concatenate(kt, axis=0).T.astype(BF16)
    ones = jnp.ones((V_ROWS - HEAD_DIM, qkv.shape[1]), F32)
    vt = []
    for h in range(N_KV_HEADS):
        vt += [qkv[Q_W + KV_W + h * HEAD_DIM:Q_W + KV_W + (h + 1) * HEAD_DIM], ones]
    v_ref[...] = jnp.concatenate(vt, axis=0).astype(BF16)

    r = _dot(u, wrest_ref[...])
    w = FNET_WIDTH
    if f_ref.shape[1] == w:
        f_ref[...] = r[:, 0:w]
    else:
        f_ref[...] = r[:, 0:w].reshape(f_ref.shape[0], -1, w).reshape(f_ref.shape)
    h_ref[...] = r[:, w:2 * w] * jax.nn.sigmoid(r[:, 2 * w:3 * w])
    gb_ref[...] = r[:, 3 * w:4 * w]
    s_ref[...] = r[:, 4 * w:5 * w] * r[:, 5 * w:6 * w]


def _inproj_call(x, mod2, g, wqkv_t, wrest, qg, kg, cos_t, sin_t):
    n, d = x.shape
    tm = _tile(n, INPROJ_ROWS)
    row = lambda i: (i, 0)
    col = lambda i: (0, i)
    fix = lambda i: (0, 0)
    w = FNET_WIDTH
    f_rows = _split_len(n)[1] if n > FOURIER_DIRECT_MAX else 1
    return pl.pallas_call(
        _inproj_kernel,
        grid=(n // tm,),
        in_specs=[
            pl.BlockSpec((tm, d), row),
            pl.BlockSpec((2, d), fix),
            pl.BlockSpec((1, d), fix),
            pl.BlockSpec((QKV_W, d), fix),
            pl.BlockSpec((d, REST_W), fix),
            pl.BlockSpec((HEAD_DIM, 1), fix),
            pl.BlockSpec((HEAD_DIM, 1), fix),
            pl.BlockSpec((HEAD_DIM // 2, tm), col),
            pl.BlockSpec((HEAD_DIM // 2, tm), col),
        ],
        out_specs=[
            pl.BlockSpec((N_HEADS * KV_W, tm), col),
            pl.BlockSpec((tm, KV_W), row),
            pl.BlockSpec((N_KV_HEADS * V_ROWS, tm), col),
            pl.BlockSpec((tm // f_rows, f_rows * w), row),
            pl.BlockSpec((tm, w), row),
            pl.BlockSpec((tm, w), row),
            pl.BlockSpec((tm, w), row),
        ],
        out_shape=[
            jax.ShapeDtypeStruct((N_HEADS * KV_W, n), BF16),
            jax.ShapeDtypeStruct((n, KV_W), BF16),
            jax.ShapeDtypeStruct((N_KV_HEADS * V_ROWS, n), BF16),
            jax.ShapeDtypeStruct((n // f_rows, f_rows * w), F32),
            jax.ShapeDtypeStruct((n, w), F32),
            jax.ShapeDtypeStruct((n, w), F32),
            jax.ShapeDtypeStruct((n, w), F32),
        ],
        compiler_params=_params(("parallel",)),
        name="in_proj",
    )(x, mod2, g, wqkv_t, wrest, qg, kg, cos_t, sin_t)


def _attn_kernel(q_ref, *refs, tk, n_sources, guarded):
    k_ref, v_ref = refs[0], refs[1]
    tail = refs[2:4] if n_sources == 2 else None
    o_ref, m_ref, acc_ref = refs[2 * n_sources:]
    n_chunks = k_ref.shape[0] // tk
    acc_ref[...] = jnp.zeros_like(acc_ref)

    def chunk_of(j):
        start = pl.multiple_of(j * tk, tk)
        return k_ref[pl.ds(start, tk), :], v_ref[:, pl.ds(start, tk)]

    def add_fast(kb, vb):
        for h in range(Q_GROUP):
            rows = slice(h * V_ROWS, (h + 1) * V_ROWS)
            s = _dot(kb, q_ref[h * KV_W:(h + 1) * KV_W, :])
            acc_ref[rows, :] += _dot(vb, jnp.exp2(s).astype(BF16))

    def add_slow(kb, vb):
        for h in range(Q_GROUP):
            rows = slice(h * V_ROWS, (h + 1) * V_ROWS)
            s = _dot(kb, q_ref[h * KV_W:(h + 1) * KV_W, :])
            m_old = m_ref[h:h + 1, :]
            m_new = jnp.maximum(m_old, jnp.max(s, axis=0, keepdims=True))
            acc_ref[rows, :] = (jnp.exp2(m_old - m_new) * acc_ref[rows, :]
                                + _dot(vb, jnp.exp2(s - m_new).astype(BF16)))
            m_ref[h:h + 1, :] = m_new

    def sweep(add):
        def chunk(j, carry):
            add(*chunk_of(j))
            return carry

        lax.fori_loop(0, n_chunks, chunk, 0)
        if tail is not None:
            add(tail[0][...], tail[1][...])

    if guarded:
        m_ref[...] = jnp.full_like(m_ref, -jnp.inf)
        sweep(add_slow)
    else:
        sweep(add_fast)

    outs = []
    for h in range(Q_GROUP):
        r0 = h * V_ROWS
        outs.append(acc_ref[r0:r0 + HEAD_DIM, :] / acc_ref[r0 + HEAD_DIM:r0 + HEAD_DIM + 1, :])
    o_ref[...] = jnp.concatenate(outs, axis=0).T.astype(BF16)


def _attn_call(safe, q_t, sources):
    nq = q_t.shape[1]
    tq = _tile(nq, ATTN_QUERIES)
    tk = _tile(sources[0][0].shape[0], ATTN_KEYS)
    gw = Q_GROUP * HEAD_DIM
    kv_specs, kv_args = [], []
    for k, v_t in sources:
        nk = k.shape[0]
        kv_specs += [pl.BlockSpec((nk, KV_W), lambda g, i: (0, 0)), pl.BlockSpec((V_ROWS, nk), lambda g, i: (g, 0))]
        kv_args += [k, v_t]
    def run(guarded):
        return pl.pallas_call(
            functools.partial(_attn_kernel, tk=tk, n_sources=len(sources), guarded=guarded),
            grid=(N_KV_HEADS, nq // tq),
            in_specs=[pl.BlockSpec((Q_GROUP * KV_W, tq), lambda g, i: (g, i))] + kv_specs,
            out_specs=pl.BlockSpec((tq, gw), lambda g, i: (i, g)),
            out_shape=jax.ShapeDtypeStruct((nq, Q_W), BF16),
            scratch_shapes=[pltpu.VMEM((8, tq), F32), pltpu.VMEM((Q_GROUP * V_ROWS, tq), F32)],
            compiler_params=_params(("parallel", "parallel")),
            name="attention_guarded" if guarded else "attention",
        )(q_t, *kv_args)

    return lax.cond(safe, lambda: run(False), lambda: run(True))


def _split_len(n):
    b = 1 << (int(math.log2(n)) // 2)
    return n // b, b


def _channel_dft(scale):
    c = np.arange(FNET_GROUP_DIM)
    ang = 2.0 * np.pi * np.outer(c, c) / FNET_GROUP_DIM
    eye = np.eye(FNET_GROUPS)
    return np.concatenate([np.kron(eye, np.cos(ang)), -np.kron(eye, np.sin(ang))], axis=1) * scale


def _stage1_tables(n):
    a, b = _split_len(n)
    c = np.arange(a)[None, :, None]
    pos = b * np.arange(a)[None, None, :] + np.arange(b)[:, None, None]
    ang = 2.0 * np.pi * ((c * pos) % n) / n
    co, si = np.cos(ang), np.sin(ang)
    return np.concatenate([np.concatenate([co, si], axis=2), np.concatenate([-si, co], axis=2)], axis=1)


def _stage2_table(n):
    a, b = _split_len(n)
    ang = 2.0 * np.pi * np.outer(np.arange(b), np.arange(b)) / b
    eye = np.eye(8)
    return np.concatenate([np.kron(np.cos(ang), eye), np.kron(np.sin(ang), eye)], axis=1)


def _dft_tables(n):
    ang = 2.0 * np.pi * np.outer(np.arange(n), np.arange(n)) / n
    return np.concatenate([np.cos(ang), np.sin(ang)], axis=1)


def _fft_kernel(f_ref, t1_ref, d_ref, t2_ref, o_ref, mid_ref, *, nb, n1):
    i = pl.program_id(0)
    w = FNET_WIDTH
    a = f_ref.shape[0]

    @pl.when(i < n1)
    def _():
        for bl in range(nb):
            z = _dot(f_ref[:, bl * w:(bl + 1) * w].astype(BF16), d_ref[...])
            zz = jnp.concatenate([z[:, :w], z[:, w:]], axis=0).astype(BF16)
            mid_ref[i * nb + bl] = _dot(t1_ref[bl], zz).reshape(2, a, w)

    @pl.when(i >= n1)
    def _():
        nbt = mid_ref.shape[0]
        c0 = pl.multiple_of((i - n1) * 8, 8)
        re = mid_ref[:, 0, pl.ds(c0, 8), :].reshape(nbt * 8, w)
        im = mid_ref[:, 1, pl.ds(c0, 8), :].reshape(nbt * 8, w)
        st = jnp.concatenate([re, im], axis=0).astype(BF16)
        o_ref[...] = _dot(t2_ref[...], st).reshape(nbt, 8, w)


def _fft_small_kernel(f_ref, t_ref, d_ref, o_ref):
    w = FNET_WIDTH
    z = _dot(f_ref[...].astype(BF16), d_ref[...])
    zz = jnp.concatenate([z[:, :w], z[:, w:]], axis=0).astype(BF16)
    o_ref[...] = _dot(t_ref[...], zz)


def _fourier_call(f, consts):
    w = FNET_WIDTH
    n = f.size // w
    if "t1" not in consts:
        return pl.pallas_call(
            _fft_small_kernel,
            out_shape=jax.ShapeDtypeStruct((n, w), F32),
            compiler_params=pltpu.CompilerParams(vmem_limit_bytes=V7X_VMEM_LIMIT_BYTES),
            name="fourier_small",
        )(f, consts["t"], consts["d"])
    a, b = _split_len(n)
    nb = 8
    n1, n2 = b // nb, a // 8
    out = pl.pallas_call(
        functools.partial(_fft_kernel, nb=nb, n1=n1),
        grid=(n1 + n2,),
        in_specs=[
            pl.BlockSpec((a, nb * w), lambda i: (0, jnp.minimum(i, n1 - 1))),
            pl.BlockSpec((nb, 2 * a, 2 * a), lambda i: (jnp.minimum(i, n1 - 1), 0, 0)),
            pl.BlockSpec((w, 2 * w), lambda i: (0, 0)),
            pl.BlockSpec((8 * b, 16 * b), lambda i: (0, 0), pipeline_mode=pl.Buffered(1)),
        ],
        out_specs=pl.BlockSpec((b, 8, w), lambda i: (0, jnp.maximum(i - n1, 0), 0)),
        out_shape=jax.ShapeDtypeStruct((b, a, w), F32),
        scratch_shapes=[pltpu.VMEM((b, 2, a, w), F32)],
        compiler_params=_params(("arbitrary",)),
        name="fourier",
    )(f, consts["t1"], consts["d"], consts["t2"])
    return out.reshape(n, w)


def _fourier_consts(n):
    scale = 1.0 / math.sqrt(n * FNET_GROUP_DIM)
    d = jnp.asarray(_channel_dft(scale), BF16)
    if n <= FOURIER_DIRECT_MAX:
        return {"d": d, "t": jnp.asarray(_dft_tables(n), BF16)}
    return {"d": d, "t1": jnp.asarray(_stage1_tables(n), BF16), "t2": jnp.asarray(_stage2_table(n), BF16)}


def _fill_window(win_ref, prev_ref, cur_ref, next_ref, tm):
    i = pl.program_id(0)
    last = pl.num_programs(0) - 1
    win_ref[0:HALO, :] = jnp.where(i > 0, prev_ref[tm - HALO:tm, :], 0.0)
    win_ref[HALO:HALO + tm, :] = cur_ref[...]
    win_ref[HALO + tm:HALO + tm + HALO, :] = jnp.where(i < last, next_ref[0:HALO, :], 0.0)


def _depthwise(win, w_ref, taps, rows):
    first = HALO - (taps - 1) // 2
    n = win.shape[0]
    acc = None
    for r in range(8):
        offs = [o for o in range(first, first + taps) if o % 8 == r]
        if not offs:
            continue
        base = win if r == 0 else pltpu.roll(win, n - r, 0)
        for o in offs:
            term = base[o - r:o - r + rows] * w_ref[o - first:o - first + 1, :]
            acc = term if acc is None else acc + term
    return acc


def _merge_kernel(x_ref, mod_ref, g_ref, attn_ref, fn_ref, hp_ref, hc_ref, hn_ref, sp_ref, sc_ref, sn_ref,
                  gb_ref, wg_ref, bg_ref, wa_ref, wf_ref, wc_ref, ws_ref, wo_ref,
                  dww_ref, dwb_ref, lng_ref, lnb_ref, scw_ref, o_ref, hwin_ref, swin_ref):
    tm, d = x_ref.shape
    _fill_window(hwin_ref, hp_ref, hc_ref, hn_ref, tm)
    _fill_window(swin_ref, sp_ref, sc_ref, sn_ref, tm)
    sub = tm // MERGE_SPLIT
    for part in range(MERGE_SPLIT):
        r0 = part * sub
        rows = slice(r0, r0 + sub)
        x = x_ref[rows, :]
        u = _modnorm(x, g_ref[...], mod_ref[0:1, :], mod_ref[1:2, :]).astype(BF16)

        c = _depthwise(hwin_ref[r0:r0 + sub + 2 * HALO, :], dww_ref, CONV_KERNEL, sub) + dwb_ref[...]
        mu = jnp.mean(c, axis=-1, keepdims=True)
        cc = c - mu
        var = jnp.mean(cc * cc, axis=-1, keepdims=True)
        conf = _silu((cc * lax.rsqrt(var + EPS)) * lng_ref[...] + lnb_ref[...])
        short = gb_ref[rows, :] * _depthwise(swin_ref[r0:r0 + sub + 2 * HALO, :], scw_ref, SC_KERNEL, sub)

        branches = (
            (attn_ref[rows, :], wa_ref),
            (fn_ref[rows, :].astype(BF16), wf_ref),
            (conf.astype(BF16), wc_ref),
            (short.astype(BF16), ws_ref),
        )
        merged = None
        for i, (val, w_ref) in enumerate(branches):
            cols = slice(i * d, (i + 1) * d)
            gate = jax.nn.sigmoid(_dot(u, wg_ref[:, cols]) + bg_ref[:, cols])
            term = gate * _dot(val, w_ref[...])
            merged = term if merged is None else merged + term
        o_ref[rows, :] = x + mod_ref[2:3, :] * _dot(merged.astype(BF16), wo_ref[...])


def _merge_call(x, mod3, g, attn, fn, hglu, sprod, gb, lw):
    n, d = x.shape
    tm = _tile(n, MERGE_ROWS)
    nt = n // tm
    row = lambda i: (i, 0)
    prev = lambda i: (jnp.maximum(i - 1, 0), 0)
    nxt = lambda i: (jnp.minimum(i + 1, nt - 1), 0)
    fix = lambda i: (0, 0)
    w = CONV_WIDTH

    def whole(arr):
        return pl.BlockSpec(arr.shape, fix, pipeline_mode=pl.Buffered(1))

    weights = (lw["wg"], lw["bg"], lw["wa"], lw["wf"], lw["wc"], lw["ws"], lw["wo"],
               lw["dww"], lw["dwb"], lw["lng"], lw["lnb"], lw["scw"])
    return pl.pallas_call(
        _merge_kernel,
        grid=(nt,),
        in_specs=[
            pl.BlockSpec((tm, d), row),
            pl.BlockSpec((3, d), fix),
            pl.BlockSpec((1, d), fix),
            pl.BlockSpec((tm, Q_W), row),
            pl.BlockSpec((tm, FNET_WIDTH), row),
            pl.BlockSpec((tm, w), prev), pl.BlockSpec((tm, w), row), pl.BlockSpec((tm, w), nxt),
            pl.BlockSpec((tm, w), prev), pl.BlockSpec((tm, w), row), pl.BlockSpec((tm, w), nxt),
            pl.BlockSpec((tm, w), row),
        ] + [whole(a) for a in weights],
        out_specs=pl.BlockSpec((tm, d), row),
        out_shape=jax.ShapeDtypeStruct((n, d), F32),
        scratch_shapes=[pltpu.VMEM((tm + 2 * HALO, w), F32), pltpu.VMEM((tm + 2 * HALO, w), F32)],
        compiler_params=_params(("arbitrary",)),
        name="mix_merge",
    )(x, mod3, g, attn, fn, hglu, hglu, hglu, sprod, sprod, sprod, gb, *weights)


def _rope_tables_t(n_tokens):
    n_rows = n_tokens // GRID_W
    row = jnp.broadcast_to(jnp.arange(n_rows)[:, None], (n_rows, GRID_W)).reshape(-1)
    col = jnp.broadcast_to(jnp.arange(GRID_W)[None, :], (n_rows, GRID_W)).reshape(-1)
    axis_dim = HEAD_DIM // 2
    inv_freq = ROPE_THETA ** (-jnp.arange(0, axis_dim, 2, dtype=F32) / axis_dim)
    pos = jnp.stack([row, col], axis=-1).astype(F32)
    ang = pos[:, :, None] * inv_freq
    ang_t = ang.reshape(n_tokens, axis_dim).T
    return jnp.cos(ang_t), jnp.sin(ang_t)


def kernel(x, c, ctx, c_ctx, w_ada, b_ada, norm_g, ffn_w13, ffn_w2, w_in, b_gate, q_norm_g, k_norm_g,
           w_attn_out, w_fnet, conv_dw_w, conv_dw_b, conv_ln_g, conv_ln_b, w_conv_out, sc_conv_w,
           w_sc_out, w_o, final_norm_g):
    batch, seq, d = x.shape
    assert batch == 1 and d == D_MODEL
    n_ctx = ctx.shape[1]
    depth = w_ada.shape[0]

    xs = x[0]
    cs = ctx[0]
    cos_x, sin_x = _rope_tables_t(seq)
    cos_c = jnp.ones((HEAD_DIM // 2, n_ctx), F32)
    sin_c = jnp.zeros((HEAD_DIM // 2, n_ctx), F32)
    fc_x = _fourier_consts(seq)
    fc_c = _fourier_consts(n_ctx)

    cc = jnp.zeros((8, d), F32).at[0].set(c[0]).at[1].set(c_ctx)
    mod = _mod_call(cc, w_ada, b_ada)
    mod = mod.reshape(depth, 8, N_MOD, d)

    wqkv_t = jnp.swapaxes(w_in[:, :, :QKV_W], 1, 2).astype(BF16)
    wrest = w_in[:, :, QKV_W:QKV_W + REST_W].astype(BF16)
    wg = w_in[:, :, QKV_W + REST_W:].astype(BF16)
    fg = final_norm_g.reshape(1, d)

    for l in range(depth):
        last = l == depth - 1
        mod_x, mod_c = mod[l, 0], mod[l, 1]
        lw = {
            "wg": wg[l], "bg": b_gate[l].reshape(1, -1),
            "wa": w_attn_out[l].astype(BF16), "wf": w_fnet[l].astype(BF16),
            "wc": w_conv_out[l].astype(BF16), "ws": w_sc_out[l].astype(BF16), "wo": w_o[l].astype(BF16),
            "dww": conv_dw_w[l], "dwb": conv_dw_b[l].reshape(1, -1),
            "lng": conv_ln_g[l].reshape(1, -1), "lnb": conv_ln_b[l].reshape(1, -1), "scw": sc_conv_w[l],
        }
        g0, g1, g2 = (norm_g[l, s].reshape(1, d) for s in range(3))
        qg = q_norm_g[l].reshape(HEAD_DIM, 1)
        kg = k_norm_g[l].reshape(HEAD_DIM, 1)

        xs, cs = _ffn_pair_call(xs, cs, mod_x[0:3], mod_c[0:3], g0, ffn_w13, ffn_w2, l, 0)

        q_x, k_x, v_x, f_x, h_x, s_x, gb_x = _inproj_call(
            xs, mod_x[3:5], g1, wqkv_t[l], wrest[l], qg, kg, cos_x, sin_x)
        q_c, k_c, v_c, f_c, h_c, s_c, gb_c = _inproj_call(
            cs, mod_c[3:5], g1, wqkv_t[l], wrest[l], qg, kg, cos_c, sin_c)

        bound = 8.0 * jnp.max(jnp.abs(q_norm_g[l])) * jnp.max(jnp.abs(k_norm_g[l]))
        safe = bound <= SAFE_SCORE_BOUND
        attn_x = _attn_call(safe, q_x, [(k_x, v_x), (k_c, v_c)])
        fn_x = _fourier_call(f_x, fc_x)
        xs = _merge_call(xs, mod_x[3:6], g1, attn_x, fn_x, h_x, s_x, gb_x, lw)

        if not last:
            attn_c = _attn_call(safe, q_c, [(k_c, v_c)])
            fn_c = _fourier_call(f_c, fc_c)
            cs = _merge_call(cs, mod_c[3:6], g1, attn_c, fn_c, h_c, s_c, gb_c, lw)

        if last:
            xs = _ffn_final_call(xs, mod_x[6:9], g2, ffn_w13, ffn_w2, l, 1, fg)
        else:
            xs, cs = _ffn_pair_call(xs, cs, mod_x[6:9], mod_c[6:9], g2, ffn_w13, ffn_w2, l, 1)

    return xs[None]
```
